```python
import jax, jax.numpy as jnp
from jax import lax
import numpy as np

D_MODEL = 1024
BATCH = 16
SEQ = 4096
DEPTH = 4

N_EVEN = (DEPTH + 1) // 2
N_ODD = DEPTH // 2
A_W = D_MODEL // 2
A_HEAD_DIM = 64
A_HEADS = A_W // A_HEAD_DIM
W_LORA = 64
A_LORA = 64
G_LORA = 128
P_A = 3 * A_W + W_LORA + A_LORA + G_LORA
LNX_EPS = 64e-5
B_W = D_MODEL // 2
B_GROUPS = 4
B_GROUP_DIM = B_W // B_GROUPS
CHUNK = 128
P_EVEN = P_A + 2 * B_W
C_W = D_MODEL // 2
CONV_WIDTH = 31
D_W = D_MODEL // 2
POOL_WINDOWS = (2, 4, 8, 16)
POOL_GROUP_DIM = D_W // len(POOL_WINDOWS)
P_ODD = 2 * C_W + D_W
FFN_HIDDEN = ((8 * D_MODEL + 3 * 256 - 1) // (3 * 256)) * 256
RMS_EPS = 1e-5
LN_EPS = 1e-5

kernel_name = "hybrid_rwkv7_gmlp_conformer_pool_trunk"

F32 = jnp.float32


def _rms_norm(x, g):
    x32 = x.astype(F32)
    y = x32 * lax.rsqrt(jnp.mean(jnp.square(x32), -1, keepdims=True) + RMS_EPS) * g
    return y.astype(x.dtype)


def _layer_norm(x, g, b):
    x32 = x.astype(F32)
    mu = jnp.mean(x32, -1, keepdims=True)
    var = jnp.mean(jnp.square(x32 - mu), -1, keepdims=True)
    return ((x32 - mu) * lax.rsqrt(var + LN_EPS) * g + b).astype(x.dtype)


def _rwkv7_scan(r, w, k, v, a, b):
    bsz, _, h, n = r.shape

    def step(state, inp):
        r_t, w_t, k_t, v_t, a_t, b_t = inp
        sa = jnp.einsum('bhvk,bhk->bhv', state, a_t)
        state = (state * w_t[:, :, None, :] + sa[..., None] * b_t[:, :, None, :]
                 + v_t[..., None] * k_t[:, :, None, :])
        y_t = jnp.einsum('bhvk,bhk->bhv', state, r_t)
        return state, y_t

    xs = tuple(jnp.swapaxes(t, 0, 1) for t in (r, w, k, v, a, b))
    s0 = jnp.zeros((bsz, h, n, n), F32)
    _, ys = lax.scan(step, s0, xs)
    return jnp.swapaxes(ys, 0, 1)


def _even_mixer(h, w_in, mu, w0, w_up, a0, a_up, g_up, k_k, k_a, r_k, lnx_g, lnx_b,
                bn_g, bn_b, sp_w, sp_b, w_out):
    bsz, s, _ = h.shape
    proj = h @ w_in
    pa, pb = proj[..., :P_A], proj[..., P_A:]
    shifted = jnp.pad(pa, ((0, 0), (1, 0), (0, 0)))[:, :s]
    pa = pa + (shifted - pa) * mu
    r, k, v, wd, ad, gd = jnp.split(
        pa, [A_W, 2 * A_W, 3 * A_W, 3 * A_W + W_LORA, 3 * A_W + W_LORA + A_LORA], axis=-1)
    w_log = -jax.nn.softplus(-(w0 + jnp.tanh(wd) @ w_up).astype(F32)) - 0.5
    decay = jnp.exp(-jnp.exp(w_log))
    a = jax.nn.sigmoid(a0 + ad @ a_up)
    g = jax.nn.sigmoid(gd) @ g_up
    heads = lambda t: t.reshape(bsz, s, A_HEADS, A_HEAD_DIM).astype(F32)
    kk = heads(k * k_k)
    kk = kk / jnp.maximum(jnp.sqrt(jnp.sum(kk * kk, -1, keepdims=True)), 1e-12)
    k = k * (1 + (a - 1) * k_a)
    rh, kh, vh, ah = heads(r), heads(k), heads(v), heads(a)
    y = _rwkv7_scan(rh, heads(decay), kh, vh, -kk, kk * ah)
    ym = jnp.mean(y, -1, keepdims=True)
    yv = jnp.mean(jnp.square(y - ym), -1, keepdims=True)
    y = ((y - ym) * lax.rsqrt(yv + LNX_EPS)).reshape(bsz, s, A_W) * lnx_g + lnx_b
    bonus = jnp.sum(rh * kh * r_k, -1, keepdims=True) * vh
    y_a = ((y + bonus.reshape(bsz, s, A_W)) * g).astype(h.dtype)
    gb = jax.nn.gelu(pb, approximate=False)
    u, z = gb[..., :B_W], gb[..., B_W:]
    z = _layer_norm(z, bn_g, bn_b)
    zc = z.reshape(bsz, s // CHUNK, CHUNK, B_GROUPS, B_GROUP_DIM)
    mask = jnp.tril(jnp.ones((CHUNK, CHUNK), sp_w.dtype))
    sz = jnp.einsum('gij,bcjgd->bcigd', sp_w * mask, zc) + jnp.transpose(sp_b)[:, :, None]
    y_b = (u * sz.reshape(bsz, s, B_W)).astype(h.dtype)
    return jnp.concatenate([y_a, y_b], axis=-1) @ w_out


def _odd_mixer(h, w_in, conv_w, conv_b, cn_g, cn_b, pool_w, pool_scale, w_out):
    bsz, s, _ = h.shape
    proj = h @ w_in
    cv, cg, d = jnp.split(proj, [C_W, 2 * C_W], axis=-1)
    glu = cv * jax.nn.sigmoid(cg)
    conv = lax.conv_general_dilated(
        glu, conv_w[:, None, :].astype(glu.dtype), window_strides=(1,),
        padding=[(CONV_WIDTH - 1, 0)], dimension_numbers=('NWC', 'WIO', 'NWC'),
        feature_group_count=C_W) + conv_b
    y_c = jax.nn.silu(_layer_norm(conv, cn_g, cn_b))
    d32 = d.astype(F32)
    cs = jnp.cumsum(d32, axis=1)
    t_idx = jnp.arange(s)
    outs = []
    for gi, win in enumerate(POOL_WINDOWS):
        lo, hi = gi * POOL_GROUP_DIM, (gi + 1) * POOL_GROUP_DIM
        c = cs[:, :, lo:hi]
        c_full = jnp.pad(c, ((0, 0), (win, 0), (0, 0)))
        total = c_full[:, win:] - c_full[:, :s]
        count = jnp.minimum(t_idx + 1, win).astype(F32)
        outs.append(total / count[None, :, None] - d32[:, :, lo:hi])
    pooled = jnp.stack(outs, axis=2).astype(h.dtype)
    y_d = jnp.einsum('bsgi,gio->bsgo', pooled, pool_w).reshape(bsz, s, D_W) * pool_scale
    return jnp.concatenate([y_c, y_d.astype(h.dtype)], axis=-1) @ w_out


def _swiglu(h, w_gate, w_up, w_down):
    return (jax.nn.silu(h @ w_gate) * (h @ w_up)) @ w_down


def setup_inputs(seed: int = 0) -> dict:
    key = jax.random.key(seed)
    keys = jax.random.split(key, 40)
    ctr = [0]

    def nk():
        ctr[0] += 1
        return keys[ctr[0] - 1]

    def nrm(shape, scale):
        return jax.random.normal(nk(), shape, F32) * scale

    def unif(shape, lo, hi):
        return jax.random.uniform(nk(), shape, F32, lo, hi)

    NE, NO, D = N_EVEN, N_ODD, D_MODEL
    return {
        "x": nrm((BATCH, SEQ, D), 1.0),
        "mix_norm_g": 1.0 + nrm((DEPTH, D), 0.02),
        "ffn_norm_g": 1.0 + nrm((DEPTH, D), 0.02),
        "final_norm_g": 1.0 + nrm((D,), 0.02),
        "ev_w_in": nrm((NE, D, P_EVEN), D ** -0.5),
        "ev_shift_mu": unif((NE, P_A), 0.0, 1.0),
        "ev_w0": unif((NE, A_W), -6.5, -1.5),
        "ev_w_up": nrm((NE, W_LORA, A_W), 0.5 * W_LORA ** -0.5),
        "ev_a0": nrm((NE, A_W), 0.1),
        "ev_a_up": nrm((NE, A_LORA, A_W), A_LORA ** -0.5),
        "ev_g_up": nrm((NE, G_LORA, A_W), G_LORA ** -0.5),
        "ev_k_k": 0.85 + nrm((NE, A_W), 0.02),
        "ev_k_a": 1.0 + nrm((NE, A_W), 0.02),
        "ev_r_k": nrm((NE, A_HEADS, A_HEAD_DIM), 0.1),
        "ev_lnx_g": 1.0 + nrm((NE, A_W), 0.02),
        "ev_lnx_b": nrm((NE, A_W), 0.02),
        "ev_bnorm_g": 1.0 + nrm((NE, B_W), 0.02),
        "ev_bnorm_b": nrm((NE, B_W), 0.02),
        "ev_spatial_w": nrm((NE, B_GROUPS, CHUNK, CHUNK), CHUNK ** -0.5),
        "ev_spatial_b": 1.0 + nrm((NE, B_GROUPS, CHUNK), 0.1),
        "ev_w_out": nrm((NE, D, D), D ** -0.5),
        "od_w_in": nrm((NO, D, P_ODD), D ** -0.5),
        "od_conv_w": nrm((NO, CONV_WIDTH, C_W), CONV_WIDTH ** -0.5),
        "od_conv_b": nrm((NO, C_W), 0.02),
        "od_cnorm_g": 1.0 + nrm((NO, C_W), 0.02),
        "od_cnorm_b": nrm((NO, C_W), 0.02),
        "od_pool_w": nrm((NO, len(POOL_WINDOWS), POOL_GROUP_DIM, POOL_GROUP_DIM), POOL_GROUP_DIM ** -0.5),
        "od_pool_scale": 1.0 + nrm((NO, D_W), 0.1),
        "od_w_out": nrm((NO, D, D), D ** -0.5),
        "ffn_w_gate": nrm((DEPTH, D, FFN_HIDDEN), D ** -0.5),
        "ffn_w_up": nrm((DEPTH, D, FFN_HIDDEN), D ** -0.5),
        "ffn_w_down": nrm((DEPTH, FFN_HIDDEN, D), FFN_HIDDEN ** -0.5),
    }


def reference(x, mix_norm_g, ffn_norm_g, final_norm_g,
              ev_w_in, ev_shift_mu, ev_w0, ev_w_up, ev_a0, ev_a_up, ev_g_up,
              ev_k_k, ev_k_a, ev_r_k, ev_lnx_g, ev_lnx_b, ev_bnorm_g, ev_bnorm_b,
              ev_spatial_w, ev_spatial_b, ev_w_out,
              od_w_in, od_conv_w, od_conv_b, od_cnorm_g, od_cnorm_b,
              od_pool_w, od_pool_scale, od_w_out,
              ffn_w_gate, ffn_w_up, ffn_w_down):
    h = x
    for layer in range(DEPTH):
        i = layer // 2
        hn = _rms_norm(h, mix_norm_g[layer])
        if layer % 2 == 0:
            mix = _even_mixer(hn, ev_w_in[i], ev_shift_mu[i], ev_w0[i], ev_w_up[i],
                              ev_a0[i], ev_a_up[i], ev_g_up[i], ev_k_k[i], ev_k_a[i],
                              ev_r_k[i], ev_lnx_g[i], ev_lnx_b[i], ev_bnorm_g[i],
                              ev_bnorm_b[i], ev_spatial_w[i], ev_spatial_b[i], ev_w_out[i])
        else:
            mix = _odd_mixer(hn, od_w_in[i], od_conv_w[i], od_conv_b[i], od_cnorm_g[i],
                             od_cnorm_b[i], od_pool_w[i], od_pool_scale[i], od_w_out[i])
        h = h + mix.astype(h.dtype)
        ff = _swiglu(_rms_norm(h, ffn_norm_g[layer]), ffn_w_gate[layer], ffn_w_up[layer],
                     ffn_w_down[layer])
        h = h + ff.astype(h.dtype)
    return _rms_norm(h, final_norm_g)
```

```python
import functools

import jax
import jax.numpy as jnp
from jax import lax
from jax.experimental import pallas as pl
from jax.experimental.pallas import tpu as pltpu

F32 = jnp.float32
BF16 = jnp.bfloat16

D_MODEL = 1024
DEPTH = 4
A_W = 512
HEAD_DIM = 64
W_LORA = 64
A_LORA = 64
G_LORA = 128
P_A = 3 * A_W + W_LORA + A_LORA + G_LORA
LNX_EPS = 64e-5
B_W = 512
B_GROUPS = 4
SP_CHUNK = 128
C_W = 512
CONV_WIDTH = 31
D_W = 512
POOL_WINDOWS = (2, 4, 8, 16)
POOL_GROUP_DIM = 128
P_ODD = 2 * C_W + D_W
FFN_HIDDEN = 2816
RMS_EPS = 1e-5
LN_EPS = 1e-5

LANES = 128
SCAN_CHUNK = 64
CONV_HALO = 32
POOL_HALO = 16
CONV_ROWS = 64
FFN_HIDDEN_CHUNK = 256
MIB = 1024 * 1024

_NN = (((1,), (0,)), ((), ()))
_NT = (((1,), (1,)), ((), ()))


def _dg(a, b, dims=_NN):
    return lax.dot_general(a, b, dims, preferred_element_type=F32)


def _split2(x):
    hi = x.astype(BF16)
    lo = (x - hi.astype(F32)).astype(BF16)
    return hi, lo


def _split3(x):
    hi = x.astype(BF16)
    r1 = x - hi.astype(F32)
    mid = r1.astype(BF16)
    lo = (r1 - mid.astype(F32)).astype(BF16)
    return hi, mid, lo


def _mm3(a, b, dims=_NN):
    return _dg(a[0], b[0], dims) + (_dg(a[0], b[1], dims) + _dg(a[1], b[0], dims))


def _rms(x, g):
    return x * lax.rsqrt(jnp.mean(x * x, axis=-1, keepdims=True) + RMS_EPS) * g


def _layer_norm(x, g, b, eps):
    mu = jnp.mean(x, axis=-1, keepdims=True)
    xc = x - mu
    var = jnp.mean(xc * xc, axis=-1, keepdims=True)
    return xc * lax.rsqrt(var + eps) * g + b


def _sigmoid(x):
    return jax.nn.sigmoid(x)


def _compiler_params(n_grid, vmem_mib):
    return pltpu.CompilerParams(
        dimension_semantics=("arbitrary",) * n_grid,
        vmem_limit_bytes=vmem_mib * MIB,
    )


def _const_spec(shape):
    nd = len(shape)
    return pl.BlockSpec(shape, lambda *_: (0,) * nd, pipeline_mode=pl.Buffered(1))


def _even_in_kernel(x_ref, g_ref, wa_ref, wb_ref, mu_ref, bng_ref, bnb_ref, spw_ref, spbt_ref,
                    pa_ref, yb_ref, shift_ref, *, tm):
    s = pl.program_id(1)
    xn = _rms(x_ref[0], g_ref[...]).astype(BF16)

    pa = _dg(xn, wa_ref[...])

    @pl.when(s == 0)
    def _():
        shift_ref[0:8, :] = jnp.zeros((8, P_A), F32)

    shift_ref[8:8 + tm, :] = pa
    shifted = shift_ref[7:7 + tm, :]
    pa_ref[0] = pa + (shifted - pa) * mu_ref[...]
    shift_ref[7:8, :] = pa[tm - 1:tm, :]

    pb = _dg(xn, wb_ref[...])
    gb = 0.5 * pb * (1.0 + lax.erf(pb * (2.0 ** -0.5)))
    u = gb[:, :B_W]
    zn = _layer_norm(gb[:, B_W:], bng_ref[...], bnb_ref[...], LN_EPS).astype(BF16)
    row = lax.broadcasted_iota(jnp.int32, (SP_CHUNK, SP_CHUNK), 0)
    col = lax.broadcasted_iota(jnp.int32, (SP_CHUNK, SP_CHUNK), 1)
    tril = row >= col
    for g in range(B_GROUPS):
        wg = jnp.where(tril, spw_ref[g], 0.0).astype(BF16)
        bias = spbt_ref[:, g:g + 1]
        cs = slice(g * LANES, (g + 1) * LANES)
        for c in range(tm // SP_CHUNK):
            rs = slice(c * SP_CHUNK, (c + 1) * SP_CHUNK)
            sz = _dg(wg, zn[rs, cs]) + bias
            yb_ref[0, rs, cs] = (u[rs, cs] * sz).astype(BF16)


def _even_in_proj(h, norm_g, wa, wb, mu, bn_g, bn_b, sp_w, sp_bt, tm):
    bsz, seq, d = h.shape
    kern = functools.partial(_even_in_kernel, tm=tm)
    return pl.pallas_call(
        kern,
        grid=(bsz, seq // tm),
        in_specs=[
            pl.BlockSpec((1, tm, d), lambda b, s: (b, s, 0)),
            _const_spec((1, d)),
            _const_spec((d, P_A)),
            _const_spec((d, 2 * B_W)),
            _const_spec((1, P_A)),
            _const_spec((1, B_W)),
            _const_spec((1, B_W)),
            _const_spec((B_GROUPS, SP_CHUNK, SP_CHUNK)),
            _const_spec((SP_CHUNK, B_GROUPS)),
        ],
        out_specs=[
            pl.BlockSpec((1, tm, P_A), lambda b, s: (b, s, 0)),
            pl.BlockSpec((1, tm, B_W), lambda b, s: (b, s, 0)),
        ],
        out_shape=[
            jax.ShapeDtypeStruct((bsz, seq, P_A), F32),
            jax.ShapeDtypeStruct((bsz, seq, B_W), BF16),
        ],
        scratch_shapes=[pltpu.VMEM((tm + 8, P_A), F32)],
        compiler_params=_compiler_params(2, 48),
        name="even_in_proj",
    )(h, norm_g, wa, wb, mu, bn_g, bn_b, sp_w, sp_bt)


def _softplus(z):
    return jnp.maximum(z, 0.0) + jnp.log1p(jnp.exp(-jnp.abs(z)))


def _rwkv_kernel(pa_ref, w0_ref, a0_ref, kk_ref, ka_ref, rk_ref, lg_ref, lb_ref,
                 wup_ref, aup_ref, gup_ref, e_ref, y_ref, s_ref):
    c = pl.program_id(1)
    C = SCAN_CHUNK
    n_pairs = A_W // LANES

    @pl.when(c == 0)
    def _():
        s_ref[...] = jnp.zeros_like(s_ref)

    pa = pa_ref[0]
    r = pa[:, 0:A_W]
    k = pa[:, A_W:2 * A_W]
    v = pa[:, 2 * A_W:3 * A_W]
    wa_d = pa[:, 3 * A_W:3 * A_W + LANES]
    gd = pa[:, 3 * A_W + LANES:P_A]

    wlin = w0_ref[...] + _mm3(_split2(jnp.tanh(wa_d)), (wup_ref[0], wup_ref[1]))
    ld = -jnp.exp(-_softplus(-wlin) - 0.5)
    a = _sigmoid(a0_ref[...] + _mm3(_split2(wa_d), (aup_ref[0], aup_ref[1])))
    gate = _mm3(_split2(_sigmoid(gd)), (gup_ref[0], gup_ref[1]))

    ones_bd = e_ref[...]

    def head_sum(x):
        p = _split3(x)
        return _dg(p[0], ones_bd) + (_dg(p[1], ones_bd) + _dg(p[2], ones_bd))

    kk = k * kk_ref[...]
    kkn = kk / jnp.maximum(jnp.sqrt(head_sum(kk * kk)), 1e-12)
    k2 = k * (1.0 + (a - 1.0) * ka_ref[...])

    rc = lax.broadcasted_iota(jnp.int32, (C, C), 0)
    cc = lax.broadcasted_iota(jnp.int32, (C, C), 1)
    lower_ones = jnp.where(rc >= cc, 1.0, 0.0).astype(BF16)
    ldp = _split3(ld)
    cum = _dg(lower_ones, ldp[0]) + (_dg(lower_ones, ldp[1]) + _dg(lower_ones, ldp[2]))
    ecum = jnp.exp(cum)
    einv = jnp.exp(-cum)
    eprev = jnp.exp(cum - ld)
    rt = r * ecum
    at = -(kkn * eprev)
    bt = kkn * a * einv
    kt = k2 * einv
    w_end = ecum[C - 1:C, :]

    row = lax.broadcasted_iota(jnp.int32, (C, LANES), 0)
    lane = lax.broadcasted_iota(jnp.int32, (C, LANES), 1)
    lane_in = jnp.bitwise_and(lane, HEAD_DIM - 1)
    m0 = lane < HEAD_DIM
    strict = lane_in < row
    incl = lane_in <= row
    eye_pair = jnp.where(lane_in == row, 1.0, 0.0)
    row2 = lax.broadcasted_iota(jnp.int32, (2 * C, LANES), 0)
    lane2 = lax.broadcasted_iota(jnp.int32, (2 * C, LANES), 1)
    mdiag = (row2 < C) == (lane2 < HEAD_DIM)
    manti = jnp.logical_not(mdiag)
    eye2 = jnp.where(row2 == lane2, 1.0, 0.0)
    zeros_c = jnp.zeros((C, LANES), F32)

    def bd(x, mask):
        return _split2(jnp.where(mask, jnp.concatenate([x, x], axis=0), 0.0))

    ys = []
    for p in range(n_pairs):
        sl = slice(p * LANES, (p + 1) * LANES)
        at_p, rt_p, bt_p, kt_p, v_p = at[:, sl], rt[:, sl], bt[:, sl], kt[:, sl], v[:, sl]
        bt_s, kt_s = _split2(bt_p), _split2(kt_p)
        rhs_bk = tuple(jnp.concatenate([x, y], axis=0) for x, y in zip(bt_s, kt_s))
        rhs_kb = tuple(jnp.concatenate([y, x], axis=0) for x, y in zip(bt_s, kt_s))
        lhs0 = jnp.concatenate([jnp.where(m0, at_p, 0.0), jnp.where(m0, rt_p, 0.0)], axis=0)
        lhs1 = jnp.concatenate([jnp.where(m0, 0.0, at_p), jnp.where(m0, 0.0, rt_p)], axis=0)
        x0 = _mm3(_split2(lhs0), rhs_bk, _NT)
        x1 = _mm3(_split2(lhs1), rhs_kb, _NT)
        a_ab = jnp.where(strict, jnp.where(m0, x0[0:C], x1[0:C]), 0.0)
        a_ak = jnp.where(strict, jnp.where(m0, x1[0:C], x0[0:C]), 0.0)
        m_rb = jnp.where(incl, jnp.where(m0, x0[C:], x1[C:]), 0.0)
        m_rk = jnp.where(incl, jnp.where(m0, x1[C:], x0[C:]), 0.0)

        t_inv = eye_pair + a_ab
        pw = _mm3(_split2(a_ab), bd(a_ab, mdiag))
        for _ in range(4):
            res = _mm3(_split2(jnp.concatenate([t_inv, pw], axis=0)), bd(pw, mdiag))
            t_inv = t_inv + res[0:C]
            pw = res[C:]
        t_inv = t_inv + _mm3(_split2(t_inv), bd(pw, mdiag))

        akv = _mm3(_split2(a_ak), bd(v_p, manti))
        t_s = _split2(t_inv)
        p_mat = _mm3(t_s, bd(at_p, mdiag))
        q_mat = _mm3(t_s, bd(akv, mdiag))
        mrb_s = _split2(m_rb)
        g_mat = rt_p + _mm3(mrb_s, bd(p_mat, mdiag))
        y_in = _mm3(mrb_s, bd(q_mat, mdiag)) + _mm3(_split2(m_rk), bd(v_p, manti))

        pt0 = jnp.concatenate([p_mat, zeros_c], axis=0).T
        qvt = jnp.concatenate([q_mat, v_p], axis=0).T
        full = _mm3(_split2(jnp.concatenate([pt0, qvt], axis=0)), rhs_bk)
        w_end_p = w_end[:, sl]
        phi_t = jnp.where(mdiag, eye2 + full[0:2 * C], 0.0) * w_end_p
        psi_t = jnp.where(m0, full[2 * C:3 * C], full[3 * C:]) * w_end_p

        s_old = s_ref[p]
        ys.append(_mm3(_split2(g_mat), bd(s_old, mdiag), _NT) + y_in)
        s_ref[p] = _mm3(_split2(s_old), _split2(phi_t)) + psi_t

    y = jnp.concatenate(ys, axis=1)
    inv_n = 1.0 / HEAD_DIM
    yc = y - head_sum(y) * inv_n
    yv = head_sum(yc * yc) * inv_n
    yn = yc * lax.rsqrt(yv + LNX_EPS) * lg_ref[...] + lb_ref[...]
    bonus = head_sum(r * k2 * rk_ref[...]) * v
    y_ref[0] = ((yn + bonus) * gate).astype(BF16)


def _rwkv_mix(pa, w0, a0, k_k, k_a, r_k, lnx_g, lnx_b, wup, aup, gup, ones_bd):
    bsz, seq, _ = pa.shape
    C = SCAN_CHUNK
    vec = _const_spec((1, A_W))
    lora = _const_spec((2, LANES, A_W))
    return pl.pallas_call(
        _rwkv_kernel,
        grid=(bsz, seq // C),
        in_specs=[pl.BlockSpec((1, C, P_A), lambda b, c: (b, c, 0)),
                  vec, vec, vec, vec, vec, vec, vec, lora, lora, lora,
                  _const_spec((A_W, A_W))],
        out_specs=pl.BlockSpec((1, C, A_W), lambda b, c: (b, c, 0)),
        out_shape=jax.ShapeDtypeStruct((bsz, seq, A_W), BF16),
        scratch_shapes=[pltpu.VMEM((A_W // LANES, C, LANES), F32)],
        compiler_params=_compiler_params(2, 32),
        name="rwkv7_mix",
    )(pa, w0, a0, k_k, k_a, r_k, lnx_g, lnx_b, wup, aup, gup, ones_bd)


def _odd_in_kernel(x_ref, g_ref, w_ref, cw_ref, cb_ref, cng_ref, cnb_ref, pw_ref, ps_ref,
                   yc_ref, yd_ref, gbuf, dbuf, *, tm):
    s = pl.program_id(1)
    xn = _rms(x_ref[0], g_ref[...]).astype(BF16)
    proj = _dg(xn, w_ref[...])
    glu = proj[:, :C_W] * _sigmoid(proj[:, C_W:2 * C_W])
    d = proj[:, 2 * C_W:]

    @pl.when(s == 0)
    def _():
        gbuf[0:CONV_HALO, :] = jnp.zeros((CONV_HALO, C_W), F32)
        dbuf[0:POOL_HALO, :] = jnp.zeros((POOL_HALO, D_W), F32)

    gbuf[CONV_HALO:CONV_HALO + tm, :] = glu
    dbuf[POOL_HALO:POOL_HALO + tm, :] = d

    first = CONV_HALO - (CONV_WIDTH - 1)
    for rt in range(tm // CONV_ROWS):
        base = rt * CONV_ROWS
        acc = jnp.broadcast_to(cb_ref[...], (CONV_ROWS, C_W))
        for j in range(CONV_WIDTH):
            acc = acc + gbuf[base + first + j:base + first + j + CONV_ROWS, :] * cw_ref[j:j + 1, :]
        ln = _layer_norm(acc, cng_ref[...], cnb_ref[...], LN_EPS)
        yc_ref[0, base:base + CONV_ROWS, :] = (ln * _sigmoid(ln)).astype(BF16)

    t_idx = s * tm + lax.broadcasted_iota(jnp.int32, (tm, 1), 0)
    for gi, win in enumerate(POOL_WINDOWS):
        cs = slice(gi * POOL_GROUP_DIM, (gi + 1) * POOL_GROUP_DIM)
        tot = dbuf[POOL_HALO:POOL_HALO + tm, cs]
        for i in range(1, win):
            tot = tot + dbuf[POOL_HALO - i:POOL_HALO - i + tm, cs]
        count = jnp.minimum(t_idx + 1, win).astype(F32)
        pooled = tot / count - d[:, cs]
        yd = _dg(pooled.astype(BF16), pw_ref[gi].astype(BF16)) * ps_ref[:, cs]
        yd_ref[0, :, cs] = yd.astype(BF16)

    gbuf[0:CONV_HALO, :] = gbuf[tm:tm + CONV_HALO, :]
    dbuf[0:POOL_HALO, :] = dbuf[tm:tm + POOL_HALO, :]


def _odd_in_proj(h, norm_g, w_in, conv_w, conv_b, cn_g, cn_b, pool_w, pool_scale, tm):
    bsz, seq, d = h.shape
    kern = functools.partial(_odd_in_kernel, tm=tm)
    n_groups = len(POOL_WINDOWS)
    return pl.pallas_call(
        kern,
        grid=(bsz, seq // tm),
        in_specs=[
            pl.BlockSpec((1, tm, d), lambda b, s: (b, s, 0)),
            _const_spec((1, d)),
            _const_spec((d, P_ODD)),
            _const_spec((CONV_WIDTH, C_W)),
            _const_spec((1, C_W)),
            _const_spec((1, C_W)),
            _const_spec((1, C_W)),
            _const_spec((n_groups, POOL_GROUP_DIM, POOL_GROUP_DIM)),
            _const_spec((1, D_W)),
        ],
        out_specs=[
            pl.BlockSpec((1, tm, C_W), lambda b, s: (b, s, 0)),
            pl.BlockSpec((1, tm, D_W), lambda b, s: (b, s, 0)),
        ],
        out_shape=[
            jax.ShapeDtypeStruct((bsz, seq, C_W), BF16),
            jax.ShapeDtypeStruct((bsz, seq, D_W), BF16),
        ],
        scratch_shapes=[pltpu.VMEM((tm + CONV_HALO, C_W), F32),
                        pltpu.VMEM((tm + POOL_HALO, D_W), F32)],
        compiler_params=_compiler_params(2, 40),
        name="odd_in_proj",
    )(h, norm_g, w_in, conv_w, conv_b, cn_g, cn_b, pool_w, pool_scale)


def _ffn_kernel(h_ref, ya_ref, yb_ref, woa_ref, wob_ref, fg_ref, wg_ref, wu_ref, wd_ref, fin_ref,
                o_ref, *, final):
    h1 = h_ref[...] + (_dg(ya_ref[...], woa_ref[...]) + _dg(yb_ref[...], wob_ref[...]))
    xn = _rms(h1, fg_ref[...]).astype(BF16)
    acc = jnp.zeros_like(h1)
    for j in range(0, FFN_HIDDEN, FFN_HIDDEN_CHUNK):
        gt = _dg(xn, wg_ref[:, j:j + FFN_HIDDEN_CHUNK])
        up = _dg(xn, wu_ref[:, j:j + FFN_HIDDEN_CHUNK])
        act = (gt * _sigmoid(gt) * up).astype(BF16)
        acc = acc + _dg(act, wd_ref[j:j + FFN_HIDDEN_CHUNK, :])
    out = h1 + acc
    if final:
        out = _rms(out, fin_ref[...])
    o_ref[...] = out


def _out_proj_ffn(h, ya, yb, wo_a, wo_b, ffn_g, w_gate, w_up, w_down, final_g, final, tm):
    t, d = h.shape
    half = ya.shape[1]
    kern = functools.partial(_ffn_kernel, final=final)
    return pl.pallas_call(
        kern,
        grid=(t // tm,),
        in_specs=[
            pl.BlockSpec((tm, d), lambda i: (i, 0)),
            pl.BlockSpec((tm, half), lambda i: (i, 0)),
            pl.BlockSpec((tm, half), lambda i: (i, 0)),
            _const_spec((half, d)),
            _const_spec((half, d)),
            _const_spec((1, d)),
            _const_spec((d, FFN_HIDDEN)),
            _const_spec((d, FFN_HIDDEN)),
            _const_spec((FFN_HIDDEN, d)),
            _const_spec((1, d)),
        ],
        out_specs=pl.BlockSpec((tm, d), lambda i: (i, 0)),
        out_shape=jax.ShapeDtypeStruct((t, d), F32),
        compiler_params=_compiler_params(1, 52),
        name="out_proj_ffn",
    )(h, ya, yb, wo_a, wo_b, ffn_g, w_gate, w_up, w_down, final_g)


def _hi_lo(w):
    hi = w.astype(BF16)
    lo = (w - hi.astype(F32)).astype(BF16)
    return jnp.stack([hi, lo])


def kernel(x, mix_norm_g, ffn_norm_g, final_norm_g, ev_w_in, ev_shift_mu, ev_w0, ev_w_up, ev_a0, ev_a_up, ev_g_up, ev_k_k, ev_k_a, ev_r_k, ev_lnx_g, ev_lnx_b, ev_bnorm_g, ev_bnorm_b, ev_spatial_w, ev_spatial_b, ev_w_out, od_w_in, od_conv_w, od_conv_b, od_cnorm_g, od_cnorm_b, od_pool_w, od_pool_scale, od_w_out, ffn_w_gate, ffn_w_up, ffn_w_down):
    bsz, seq, d = x.shape
    tm = min(512, seq)
    row = lambda vec: vec.reshape(1, -1)
    head_id = jnp.arange(A_W) // HEAD_DIM
    ones_bd = (head_id[:, None] == head_id[None, :]).astype(BF16)
    zpad = jnp.zeros((LANES - W_LORA, A_W), F32)

    h = x
    for layer in range(DEPTH):
        i = layer // 2
        if layer % 2 == 0:
            w_in = ev_w_in[i].astype(BF16)
            pa, y2 = _even_in_proj(
                h, row(mix_norm_g[layer]), w_in[:, :P_A], w_in[:, P_A:], row(ev_shift_mu[i]),
                row(ev_bnorm_g[i]), row(ev_bnorm_b[i]), ev_spatial_w[i], ev_spatial_b[i].T, tm)
            y1 = _rwkv_mix(
                pa, row(ev_w0[i]), row(ev_a0[i]), row(ev_k_k[i]), row(ev_k_a[i]), row(ev_r_k[i]),
                row(ev_lnx_g[i]), row(ev_lnx_b[i]),
                _hi_lo(jnp.concatenate([ev_w_up[i], zpad], axis=0)),
                _hi_lo(jnp.concatenate([zpad, ev_a_up[i]], axis=0)),
                _hi_lo(ev_g_up[i]), ones_bd)
            w_out = ev_w_out[i].astype(BF16)
        else:
            y1, y2 = _odd_in_proj(
                h, row(mix_norm_g[layer]), od_w_in[i].astype(BF16), od_conv_w[i], row(od_conv_b[i]),
                row(od_cnorm_g[i]), row(od_cnorm_b[i]), od_pool_w[i], row(od_pool_scale[i]), tm)
            w_out = od_w_out[i].astype(BF16)
        half = d // 2
        h = _out_proj_ffn(
            h.reshape(bsz * seq, d), y1.reshape(bsz * seq, half), y2.reshape(bsz * seq, half),
            w_out[:half], w_out[half:], row(ffn_norm_g[layer]),
            ffn_w_gate[layer].astype(BF16), ffn_w_up[layer].astype(BF16),
            ffn_w_down[layer].astype(BF16), row(final_norm_g),
            layer == DEPTH - 1, tm).reshape(bsz, seq, d)
    return h
```

```python
import functools

import jax
import jax.numpy as jnp
from jax import lax
from jax.experimental import pallas as pl
from jax.experimental.pallas import tpu as pltpu

F32 = jnp.float32
BF16 = jnp.bfloat16

D_MODEL = 1024
DEPTH = 4
A_W = 512
HEAD_DIM = 64
W_LORA = 64
A_LORA = 64
G_LORA = 128
P_A = 3 * A_W + W_LORA + A_LORA + G_LORA
LNX_EPS = 64e-5
B_W = 512
B_GROUPS = 4
SP_CHUNK = 128
C_W = 512
CONV_WIDTH = 31
D_W = 512
POOL_WINDOWS = (2, 4, 8, 16)
POOL_GROUP_DIM = 128
P_ODD = 2 * C_W + D_W
FFN_HIDDEN = 2816
RMS_EPS = 1e-5
LN_EPS = 1e-5

LANES = 128
SCAN_CHUNK = 64
SCAN_CHUNK_LOG2 = SCAN_CHUNK.bit_length() - 1
SCAN_CHUNKS_PER_STEP = 4
CONV_HALO = 32
POOL_HALO = 16
CONV_ROWS = 64
FFN_HIDDEN_CHUNK = 256
MIB = 1024 * 1024

_NN = (((1,), (0,)), ((), ()))
_NT = (((1,), (1,)), ((), ()))

_PARTS = {
    "gram": (1, 1),
    "inv": (1, 1),
    "akv": (1, 1),
    "pq": (1, 1),
    "gy": (1, 1),
    "trans": (2, 2),
    "out": (1, 1),
    "state": (2, 2),
}


def _dg(a, b, dims=_NN):
    return lax.dot_general(a, b, dims, preferred_element_type=F32)


def _split2(x):
    hi = x.astype(BF16)
    lo = (x - hi.astype(F32)).astype(BF16)
    return hi, lo


def _split3(x):
    hi = x.astype(BF16)
    r1 = x - hi.astype(F32)
    mid = r1.astype(BF16)
    lo = (r1 - mid.astype(F32)).astype(BF16)
    return hi, mid, lo


def _parts(x, n):
    return (x.astype(BF16),) if n == 1 else _split2(x)


def _mmp(a, b, dims=_NN):
    if len(a) == 2:
        m = a[0].shape[0]
        both = _dg(jnp.concatenate(a, axis=0), b[0], dims)
        out = both[:m] + both[m:]
    else:
        out = _dg(a[0], b[0], dims)
    if len(b) == 2:
        out = out + _dg(a[0], b[1], dims)
    return out


def _rms(x, g):
    return x * lax.rsqrt(jnp.mean(x * x, axis=-1, keepdims=True) + RMS_EPS) * g


def _layer_norm(x, g, b, eps):
    mu = jnp.mean(x, axis=-1, keepdims=True)
    xc = x - mu
    var = jnp.mean(xc * xc, axis=-1, keepdims=True)
    return xc * lax.rsqrt(var + eps) * g + b


def _sigmoid(x):
    return jax.nn.sigmoid(x)


def _compiler_params(n_grid, vmem_mib):
    return pltpu.CompilerParams(
        dimension_semantics=("arbitrary",) * n_grid,
        vmem_limit_bytes=vmem_mib * MIB,
    )


def _const_spec(shape):
    nd = len(shape)
    return pl.BlockSpec(shape, lambda *_: (0,) * nd, pipeline_mode=pl.Buffered(1))


def _even_in_kernel(x_ref, g_ref, wa_ref, wb_ref, mu_ref, bng_ref, bnb_ref, spw_ref, spbt_ref,
                    pa_ref, yb_ref, shift_ref, *, tm):
    s = pl.program_id(1)
    xn = _rms(x_ref[0], g_ref[...]).astype(BF16)

    pa = _dg(xn, wa_ref[...])

    @pl.when(s == 0)
    def _():
        shift_ref[0:8, :] = jnp.zeros((8, P_A), F32)

    shift_ref[8:8 + tm, :] = pa
    shifted = shift_ref[7:7 + tm, :]
    pa_ref[0] = pa + (shifted - pa) * mu_ref[...]
    shift_ref[7:8, :] = pa[tm - 1:tm, :]

    pb = _dg(xn, wb_ref[...])
    gb = 0.5 * pb * (1.0 + lax.erf(pb * (2.0 ** -0.5)))
    u = gb[:, :B_W]
    zn = _layer_norm(gb[:, B_W:], bng_ref[...], bnb_ref[...], LN_EPS).astype(BF16)
    row = lax.broadcasted_iota(jnp.int32, (SP_CHUNK, SP_CHUNK), 0)
    col = lax.broadcasted_iota(jnp.int32, (SP_CHUNK, SP_CHUNK), 1)
    tril = row >= col
    for g in range(B_GROUPS):
        wg = jnp.where(tril, spw_ref[g], 0.0).astype(BF16)
        bias = spbt_ref[:, g:g + 1]
        cs = slice(g * LANES, (g + 1) * LANES)
        for c in range(tm // SP_CHUNK):
            rs = slice(c * SP_CHUNK, (c + 1) * SP_CHUNK)
            sz = _dg(wg, zn[rs, cs]) + bias
            yb_ref[0, rs, cs] = (u[rs, cs] * sz).astype(BF16)


def _even_in_proj(h, norm_g, wa, wb, mu, bn_g, bn_b, sp_w, sp_bt, tm):
    bsz, seq, d = h.shape
    kern = functools.partial(_even_in_kernel, tm=tm)
    return pl.pallas_call(
        kern,
        grid=(bsz, seq // tm),
        in_specs=[
            pl.BlockSpec((1, tm, d), lambda b, s: (b, s, 0)),
            _const_spec((1, d)),
            _const_spec((d, P_A)),
            _const_spec((d, 2 * B_W)),
            _const_spec((1, P_A)),
            _const_spec((1, B_W)),
            _const_spec((1, B_W)),
            _const_spec((B_GROUPS, SP_CHUNK, SP_CHUNK)),
            _const_spec((SP_CHUNK, B_GROUPS)),
        ],
        out_specs=[
            pl.BlockSpec((1, tm, P_A), lambda b, s: (b, s, 0)),
            pl.BlockSpec((1, tm, B_W), lambda b, s: (b, s, 0)),
        ],
        out_shape=[
            jax.ShapeDtypeStruct((bsz, seq, P_A), F32),
            jax.ShapeDtypeStruct((bsz, seq, B_W), BF16),
        ],
        scratch_shapes=[pltpu.VMEM((tm + 8, P_A), F32)],
        compiler_params=_compiler_params(2, 48),
        name="even_in_proj",
    )(h, norm_g, wa, wb, mu, bn_g, bn_b, sp_w, sp_bt)


def _rwkv_kernel(pa_ref, w0_ref, a0_ref, kk_ref, ka_ref, rk_ref, lg_ref, lb_ref,
                 wup_ref, aup_ref, gup_ref, y_ref, s_ref, *, nch):
    C = SCAN_CHUNK
    tb = nch * C
    n_pairs = A_W // LANES

    @pl.when(pl.program_id(1) == 0)
    def _():
        s_ref[...] = jnp.zeros_like(s_ref)

    pa = pa_ref[0]
    r = pa[:, 0:A_W]
    k = pa[:, A_W:2 * A_W]
    v = pa[:, 2 * A_W:3 * A_W]
    wa_d = pa[:, 3 * A_W:3 * A_W + LANES]
    gd = pa[:, 3 * A_W + LANES:P_A]

    wlin = w0_ref[...] + _mmp(_split2(jnp.tanh(wa_d)), (wup_ref[0], wup_ref[1]))
    ld = -(2.718281828459045 ** -0.5) * _sigmoid(wlin)
    a = _sigmoid(a0_ref[...] + _dg(wa_d.astype(BF16), aup_ref[...]))
    gate = _dg(_sigmoid(gd).astype(BF16), gup_ref[...])

    row = lax.broadcasted_iota(jnp.int32, (C, LANES), 0)
    lane = lax.broadcasted_iota(jnp.int32, (C, LANES), 1)
    lane_in = jnp.bitwise_and(lane, HEAD_DIM - 1)
    m0 = lane < HEAD_DIM
    strict = lane_in < row
    incl = lane_in <= row
    eye_pair = jnp.where(lane_in == row, 1.0, 0.0)
    row2 = lax.broadcasted_iota(jnp.int32, (2 * C, LANES), 0)
    lane2 = lax.broadcasted_iota(jnp.int32, (2 * C, LANES), 1)
    mdiag = (row2 < C) == (lane2 < HEAD_DIM)
    mdiag_wide = jnp.concatenate([mdiag, mdiag], axis=1)
    eye2 = jnp.where(row2 == lane2, 1.0, 0.0)
    row4 = lax.broadcasted_iota(jnp.int32, (4 * C, LANES), 0)
    lane4 = lax.broadcasted_iota(jnp.int32, (4 * C, LANES), 1)
    mask4 = (jnp.bitwise_and(jnp.right_shift(row4, SCAN_CHUNK_LOG2), 1) == 0) == (lane4 < HEAD_DIM)
    zeros_c = jnp.zeros((C, LANES), F32)
    ones_pair = jnp.where(mdiag, 1.0, 0.0).astype(BF16)

    def head_sum(x):
        parts = _split2(x)
        stacked = jnp.concatenate(
            [q[:, p * LANES:(p + 1) * LANES] for q in parts for p in range(n_pairs)], axis=0)
        s = _dg(stacked, ones_pair)
        s = s[:n_pairs * tb] + s[n_pairs * tb:]
        return jnp.concatenate([s[p * tb:(p + 1) * tb] for p in range(n_pairs)], axis=1)

    kk = k * kk_ref[...]
    kkn = kk / jnp.maximum(jnp.sqrt(head_sum(kk * kk)), 1e-12)
    k2 = k * (1.0 + (a - 1.0) * ka_ref[...])

    rc = lax.broadcasted_iota(jnp.int32, (tb, tb), 0)
    cc = lax.broadcasted_iota(jnp.int32, (tb, tb), 1)
    same_chunk = jnp.right_shift(rc, SCAN_CHUNK_LOG2) == jnp.right_shift(cc, SCAN_CHUNK_LOG2)
    lower_ones = jnp.where(jnp.logical_and(rc >= cc, same_chunk), 1.0, 0.0).astype(BF16)
    ldp = _split3(ld)
    cum = _dg(lower_ones, ldp[0]) + (_dg(lower_ones, ldp[1]) + _dg(lower_ones, ldp[2]))
    ecum = jnp.exp(cum)
    einv = jnp.exp(-cum)
    eprev = jnp.exp(cum - ld)
    rt = r * ecum
    at = -(kkn * eprev)
    bt = kkn * a * einv
    kt = k2 * einv

    def bd(x, n):
        return _parts(jnp.where(mdiag, jnp.concatenate([x, x], axis=0), 0.0), n)

    chains = [(j, p) for j in range(nch) for p in range(n_pairs)]

    def tile(x, jp):
        j, p = jp
        return x[j * C:(j + 1) * C, p * LANES:(p + 1) * LANES]

    at_c = [tile(at, jp) for jp in chains]
    rt_c = [tile(rt, jp) for jp in chains]
    bt_c = [tile(bt, jp) for jp in chains]
    kt_c = [tile(kt, jp) for jp in chains]
    v_c = [tile(v, jp) for jp in chains]

    nl, nr = _PARTS["gram"]
    gram = [
        _mmp(_parts(jnp.concatenate([at_c[i], rt_c[i]], axis=0), nl),
             _parts(jnp.where(mask4, jnp.concatenate([bt_c[i], bt_c[i], kt_c[i], kt_c[i]], axis=0), 0.0), nr),
             _NT)
        for i in range(len(chains))]
    a_ab = [jnp.where(strict, g[0:C, 0:LANES], 0.0) for g in gram]
    a_ak = [jnp.where(strict, g[0:C, LANES:], 0.0) for g in gram]
    m_rb = [jnp.where(incl, g[C:, 0:LANES], 0.0) for g in gram]
    m_rk = [jnp.where(incl, g[C:, LANES:], 0.0) for g in gram]

    nl, nr = _PARTS["akv"]
    akv = [_mmp(_parts(x, nl), bd(y, nr)) for x, y in zip(a_ak, v_c)]

    nl, nr = _PARTS["inv"]
    t_inv = [eye_pair + x for x in a_ab]
    pw = [_mmp(_parts(x, nl), bd(x, nr)) for x in a_ab]
    for _ in range(SCAN_CHUNK_LOG2 - 2):
        res = [_mmp(_parts(jnp.concatenate([t, w], axis=0), nl), bd(w, nr)) for t, w in zip(t_inv, pw)]
        t_inv = [t + x[0:C] for t, x in zip(t_inv, res)]
        pw = [x[C:] for x in res]
    t_inv = [t + _mmp(_parts(t, nl), bd(w, nr)) for t, w in zip(t_inv, pw)]

    nl, nr = _PARTS["pq"]
    pq = [
        _mmp(_parts(t, nl),
             _parts(jnp.where(mdiag_wide, jnp.concatenate([jnp.concatenate([x, y], axis=1)] * 2, axis=0), 0.0), nr))
        for t, x, y in zip(t_inv, at_c, akv)]
    nl, nr = _PARTS["gy"]
    gy = [
        _mmp(_parts(m, nl), _parts(jnp.where(mdiag_wide, jnp.concatenate([x, x], axis=0), 0.0), nr))
        for m, x in zip(m_rb, pq)]
    g_mat = [x + g[:, 0:LANES] for x, g in zip(rt_c, gy)]
    y_in = [g[:, LANES:] + _mmp(_parts(m, nl), bd(y, nr)) for g, m, y in zip(gy, m_rk, v_c)]

    nl, nr = _PARTS["trans"]
    phi_t, psi_t = [], []
    for i, (j, p) in enumerate(chains):
        pt0 = jnp.concatenate([pq[i][:, 0:LANES], zeros_c], axis=0).T
        qvt = jnp.concatenate([pq[i][:, LANES:], v_c[i]], axis=0).T
        bk = _parts(jnp.concatenate([bt_c[i], kt_c[i]], axis=0), nr)
        full = _mmp(_parts(jnp.concatenate([pt0, qvt], axis=0), nl), bk)
        w_end = ecum[(j + 1) * C - 1:(j + 1) * C, p * LANES:(p + 1) * LANES]
        phi_t.append(jnp.where(mdiag, eye2 + full[0:2 * C], 0.0) * w_end)
        psi_t.append(jnp.where(m0, full[2 * C:3 * C], full[3 * C:]) * w_end)

    state = [s_ref[p] for p in range(n_pairs)]
    ol, orr = _PARTS["out"]
    sl, sr = _PARTS["state"]
    y_rows = []
    for j in range(nch):
        idx = [j * n_pairs + p for p in range(n_pairs)]
        y_rows.append(jnp.concatenate(
            [_mmp(_parts(g_mat[i], ol), bd(state[p], orr), _NT) + y_in[i] for p, i in enumerate(idx)], axis=1))
        state = [_mmp(_parts(state[p], sl), _parts(phi_t[i], sr)) + psi_t[i] for p, i in enumerate(idx)]
    for p in range(n_pairs):
        s_ref[p] = state[p]

    y = jnp.concatenate(y_rows, axis=0)
    inv_n = 1.0 / HEAD_DIM
    yc = y - head_sum(y) * inv_n
    yv = head_sum(yc * yc) * inv_n
    yn = yc * lax.rsqrt(yv + LNX_EPS) * lg_ref[...] + lb_ref[...]
    bonus = head_sum(r * k2 * rk_ref[...]) * v
    y_ref[0] = ((yn + bonus) * gate).astype(BF16)


def _rwkv_mix(pa, w0, a0, k_k, k_a, r_k, lnx_g, lnx_b, wup, aup, gup):
    bsz, seq, _ = pa.shape
    nch = SCAN_CHUNKS_PER_STEP
    tb = nch * SCAN_CHUNK
    vec = _const_spec((1, A_W))
    lora = _const_spec((LANES, A_W))
    return pl.pallas_call(
        functools.partial(_rwkv_kernel, nch=nch),
        grid=(bsz, seq // tb),
        in_specs=[pl.BlockSpec((1, tb, P_A), lambda b, c: (b, c, 0)),
                  vec, vec, vec, vec, vec, vec, vec, _const_spec((2, LANES, A_W)), lora, lora],
        out_specs=pl.BlockSpec((1, tb, A_W), lambda b, c: (b, c, 0)),
        out_shape=jax.ShapeDtypeStruct((bsz, seq, A_W), BF16),
        scratch_shapes=[pltpu.VMEM((A_W // LANES, SCAN_CHUNK, LANES), F32)],
        compiler_params=_compiler_params(2, 32),
        name="rwkv7_mix",
    )(pa, w0, a0, k_k, k_a, r_k, lnx_g, lnx_b, wup, aup, gup)


def _odd_in_kernel(x_ref, g_ref, w_ref, cw_ref, cb_ref, cng_ref, cnb_ref, pw_ref, ps_ref,
                   yc_ref, yd_ref, gbuf, dbuf, *, tm):
    s = pl.program_id(1)
    xn = _rms(x_ref[0], g_ref[...]).astype(BF16)
    proj = _dg(xn, w_ref[...])
    glu = proj[:, :C_W] * _sigmoid(proj[:, C_W:2 * C_W])
    d = proj[:, 2 * C_W:]

    @pl.when(s == 0)
    def _():
        gbuf[0:CONV_HALO, :] = jnp.zeros((CONV_HALO, C_W), F32)
        dbuf[0:POOL_HALO, :] = jnp.zeros((POOL_HALO, D_W), F32)

    gbuf[CONV_HALO:CONV_HALO + tm, :] = glu
    dbuf[POOL_HALO:POOL_HALO + tm, :] = d

    first = CONV_HALO - (CONV_WIDTH - 1)
    for rt in range(tm // CONV_ROWS):
        base = rt * CONV_ROWS
        acc = jnp.broadcast_to(cb_ref[...], (CONV_ROWS, C_W))
        for j in range(CONV_WIDTH):
            acc = acc + gbuf[base + first + j:base + first + j + CONV_ROWS, :] * cw_ref[j:j + 1, :]
        ln = _layer_norm(acc, cng_ref[...], cnb_ref[...], LN_EPS)
        yc_ref[0, base:base + CONV_ROWS, :] = (ln * _sigmoid(ln)).astype(BF16)

    t_idx = s * tm + lax.broadcasted_iota(jnp.int32, (tm, 1), 0)
    for gi, win in enumerate(POOL_WINDOWS):
        cs = slice(gi * POOL_GROUP_DIM, (gi + 1) * POOL_GROUP_DIM)
        tot = dbuf[POOL_HALO:POOL_HALO + tm, cs]
        for i in range(1, win):
            tot = tot + dbuf[POOL_HALO - i:POOL_HALO - i + tm, cs]
        count = jnp.minimum(t_idx + 1, win).astype(F32)
        pooled = tot / count - d[:, cs]
        yd = _dg(pooled.astype(BF16), pw_ref[gi].astype(BF16)) * ps_ref[:, cs]
        yd_ref[0, :, cs] = yd.astype(BF16)

    gbuf[0:CONV_HALO, :] = gbuf[tm:tm + CONV_HALO, :]
    dbuf[0:POOL_HALO, :] = dbuf[tm:tm + POOL_HALO, :]


def _odd_in_proj(h, norm_g, w_in, conv_w, conv_b, cn_g, cn_b, pool_w, pool_scale, tm):
    bsz, seq, d = h.shape
    kern = functools.partial(_odd_in_kernel, tm=tm)
    n_groups = len(POOL_WINDOWS)
    return pl.pallas_call(
        kern,
        grid=(bsz, seq // tm),
        in_specs=[
            pl.BlockSpec((1, tm, d), lambda b, s: (b, s, 0)),
            _const_spec((1, d)),
            _const_spec((d, P_ODD)),
            _const_spec((CONV_WIDTH, C_W)),
            _const_spec((1, C_W)),
            _const_spec((1, C_W)),
            _const_spec((1, C_W)),
            _const_spec((n_groups, POOL_GROUP_DIM, POOL_GROUP_DIM)),
            _const_spec((1, D_W)),
        ],
        out_specs=[
            pl.BlockSpec((1, tm, C_W), lambda b, s: (b, s, 0)),
            pl.BlockSpec((1, tm, D_W), lambda b, s: (b, s, 0)),
        ],
        out_shape=[
            jax.ShapeDtypeStruct((bsz, seq, C_W), BF16),
            jax.ShapeDtypeStruct((bsz, seq, D_W), BF16),
        ],
        scratch_shapes=[pltpu.VMEM((tm + CONV_HALO, C_W), F32),
                        pltpu.VMEM((tm + POOL_HALO, D_W), F32)],
        compiler_params=_compiler_params(2, 40),
        name="odd_in_proj",
    )(h, norm_g, w_in, conv_w, conv_b, cn_g, cn_b, pool_w, pool_scale)


def _ffn_kernel(h_ref, ya_ref, yb_ref, woa_ref, wob_ref, fg_ref, wg_ref, wu_ref, wd_ref, fin_ref,
                o_ref, *, final):
    h1 = h_ref[...] + (_dg(ya_ref[...], woa_ref[...]) + _dg(yb_ref[...], wob_ref[...]))
    xn = _rms(h1, fg_ref[...]).astype(BF16)
    acc = jnp.zeros_like(h1)
    for j in range(0, FFN_HIDDEN, FFN_HIDDEN_CHUNK):
        gt = _dg(xn, wg_ref[:, j:j + FFN_HIDDEN_CHUNK])
        up = _dg(xn, wu_ref[:, j:j + FFN_HIDDEN_CHUNK])
        act = (gt * _sigmoid(gt) * up).astype(BF16)
        acc = acc + _dg(act, wd_ref[j:j + FFN_HIDDEN_CHUNK, :])
    out = h1 + acc
    if final:
        out = _rms(out, fin_ref[...])
    o_ref[...] = out


def _out_proj_ffn(h, ya, yb, wo_a, wo_b, ffn_g, w_gate, w_up, w_down, final_g, final, tm):
    t, d = h.shape
    half = ya.shape[1]
    kern = functools.partial(_ffn_kernel, final=final)
    return pl.pallas_call(
        kern,
        grid=(t // tm,),
        in_specs=[
            pl.BlockSpec((tm, d), lambda i: (i, 0)),
            pl.BlockSpec((tm, half), lambda i: (i, 0)),
            pl.BlockSpec((tm, half), lambda i: (i, 0)),
            _const_spec((half, d)),
            _const_spec((half, d)),
            _const_spec((1, d)),
            _const_spec((d, FFN_HIDDEN)),
            _const_spec((d, FFN_HIDDEN)),
            _const_spec((FFN_HIDDEN, d)),
            _const_spec((1, d)),
        ],
        out_specs=pl.BlockSpec((tm, d), lambda i: (i, 0)),
        out_shape=jax.ShapeDtypeStruct((t, d), F32),
        compiler_params=_compiler_params(1, 52),
        name="out_proj_ffn",
    )(h, ya, yb, wo_a, wo_b, ffn_g, w_gate, w_up, w_down, final_g)


def _hi_lo(w):
    hi = w.astype(BF16)
    lo = (w - hi.astype(F32)).astype(BF16)
    return jnp.stack([hi, lo])


def kernel(x, mix_norm_g, ffn_norm_g, final_norm_g, ev_w_in, ev_shift_mu, ev_w0, ev_w_up, ev_a0, ev_a_up, ev_g_up, ev_k_k, ev_k_a, ev_r_k, ev_lnx_g, ev_lnx_b, ev_bnorm_g, ev_bnorm_b, ev_spatial_w, ev_spatial_b, ev_w_out, od_w_in, od_conv_w, od_conv_b, od_cnorm_g, od_cnorm_b, od_pool_w, od_pool_scale, od_w_out, ffn_w_gate, ffn_w_up, ffn_w_down):
    bsz, seq, d = x.shape
    tm = min(512, seq)
    row = lambda vec: vec.reshape(1, -1)
    zpad = jnp.zeros((LANES - W_LORA, A_W), F32)

    h = x
    for layer in range(DEPTH):
        i = layer // 2
        if layer % 2 == 0:
            w_in = ev_w_in[i].astype(BF16)
            pa, y2 = _even_in_proj(
                h, row(mix_norm_g[layer]), w_in[:, :P_A], w_in[:, P_A:], row(ev_shift_mu[i]),
                row(ev_bnorm_g[i]), row(ev_bnorm_b[i]), ev_spatial_w[i], ev_spatial_b[i].T, tm)
            y1 = _rwkv_mix(
                pa, row(ev_w0[i]), row(ev_a0[i]), row(ev_k_k[i]), row(ev_k_a[i]), row(ev_r_k[i]),
                row(ev_lnx_g[i]), row(ev_lnx_b[i]),
                _hi_lo(jnp.concatenate([ev_w_up[i], zpad], axis=0)),
                jnp.concatenate([zpad, ev_a_up[i]], axis=0).astype(BF16),
                ev_g_up[i].astype(BF16))
            w_out = ev_w_out[i].astype(BF16)
        else:
            y1, y2 = _odd_in_proj(
                h, row(mix_norm_g[layer]), od_w_in[i].astype(BF16), od_conv_w[i], row(od_conv_b[i]),
                row(od_cnorm_g[i]), row(od_cnorm_b[i]), od_pool_w[i], row(od_pool_scale[i]), tm)
            w_out = od_w_out[i].astype(BF16)
        half = d // 2
        h = _out_proj_ffn(
            h.reshape(bsz * seq, d), y1.reshape(bsz * seq, half), y2.reshape(bsz * seq, half),
            w_out[:half], w_out[half:], row(ffn_norm_g[layer]),
            ffn_w_gate[layer].astype(BF16), ffn_w_up[layer].astype(BF16),
            ffn_w_down[layer].astype(BF16), row(final_norm_g),
            layer == DEPTH - 1, tm).reshape(bsz, seq, d)
    return h
```

```python
import functools

import jax
import jax.numpy as jnp
from jax import lax
from jax.experimental import pallas as pl
from jax.experimental.pallas import tpu as pltpu

F32 = jnp.float32
BF16 = jnp.bfloat16

D_MODEL = 1024
DEPTH = 4
A_W = 512
HEAD_DIM = 64
W_LORA = 64
A_LORA = 64
G_LORA = 128
P_A = 3 * A_W + W_LORA + A_LORA + G_LORA
LNX_EPS = 64e-5
B_W = 512
B_GROUPS = 4
SP_CHUNK = 128
C_W = 512
CONV_WIDTH = 31
D_W = 512
POOL_WINDOWS = (2, 4, 8, 16)
POOL_GROUP_DIM = 128
P_ODD = 2 * C_W + D_W
FFN_HIDDEN = 2816
RMS_EPS = 1e-5
LN_EPS = 1e-5

LANES = 128
SCAN_CHUNK = 64
SCAN_CHUNK_LOG2 = SCAN_CHUNK.bit_length() - 1
SCAN_CHUNKS_PER_STEP = 4
CONV_HALO = 32
POOL_HALO = 16
CONV_ROWS = 128
SUBLANES = 8
FFN_HIDDEN_CHUNK = 256
MIB = 1024 * 1024

_NN = (((1,), (0,)), ((), ()))
_NT = (((1,), (1,)), ((), ()))

_PARTS = {
    "gram": (1, 1),
    "inv": (1, 1),
    "akv": (1, 1),
    "pq": (1, 1),
    "gy": (1, 1),
    "trans": (2, 2),
    "out": (1, 1),
    "state": (2, 2),
}


def _dg(a, b, dims=_NN):
    return lax.dot_general(a, b, dims, preferred_element_type=F32)


def _split2(x):
    hi = x.astype(BF16)
    lo = (x - hi.astype(F32)).astype(BF16)
    return hi, lo


def _split3(x):
    hi = x.astype(BF16)
    r1 = x - hi.astype(F32)
    mid = r1.astype(BF16)
    lo = (r1 - mid.astype(F32)).astype(BF16)
    return hi, mid, lo


def _parts(x, n):
    return (x.astype(BF16),) if n == 1 else _split2(x)


def _mmp(a, b, dims=_NN):
    if len(a) == 2:
        m = a[0].shape[0]
        both = _dg(jnp.concatenate(a, axis=0), b[0], dims)
        out = both[:m] + both[m:]
    else:
        out = _dg(a[0], b[0], dims)
    if len(b) == 2:
        out = out + _dg(a[0], b[1], dims)
    return out


def _rms(x, g):
    return x * lax.rsqrt(jnp.mean(x * x, axis=-1, keepdims=True) + RMS_EPS) * g


def _layer_norm(x, g, b, eps):
    mu = jnp.mean(x, axis=-1, keepdims=True)
    xc = x - mu
    var = jnp.mean(xc * xc, axis=-1, keepdims=True)
    return xc * lax.rsqrt(var + eps) * g + b


def _sigmoid(x):
    return jax.nn.sigmoid(x)


def _compiler_params(n_grid, vmem_mib):
    return pltpu.CompilerParams(
        dimension_semantics=("arbitrary",) * n_grid,
        vmem_limit_bytes=vmem_mib * MIB,
    )


def _const_spec(shape):
    nd = len(shape)
    return pl.BlockSpec(shape, lambda *_: (0,) * nd, pipeline_mode=pl.Buffered(1))


def _even_in_kernel(x_ref, g_ref, wa_ref, wb_ref, mu_ref, bng_ref, bnb_ref, spw_ref, spbt_ref,
                    pa_ref, yb_ref, shift_ref, *, tm):
    @pl.when(pl.program_id(1) == 0)
    def _():
        shift_ref[0:8, :] = jnp.zeros((8, P_A), F32)

    xn = _rms(x_ref[0], g_ref[...]).astype(BF16)

    pb = _dg(xn, wb_ref[...])
    gb = 0.5 * pb * (1.0 + lax.erf(pb * (2.0 ** -0.5)))
    u = gb[:, :B_W]
    zn = _layer_norm(gb[:, B_W:], bng_ref[...], bnb_ref[...], LN_EPS).astype(BF16)
    row = lax.broadcasted_iota(jnp.int32, (SP_CHUNK, SP_CHUNK), 0)
    col = lax.broadcasted_iota(jnp.int32, (SP_CHUNK, SP_CHUNK), 1)
    tril = row >= col
    for g in range(B_GROUPS):
        wg = jnp.where(tril, spw_ref[g], 0.0).astype(BF16)
        bias = spbt_ref[:, g:g + 1]
        cs = slice(g * LANES, (g + 1) * LANES)
        for c in range(tm // SP_CHUNK):
            rs = slice(c * SP_CHUNK, (c + 1) * SP_CHUNK)
            sz = _dg(wg, zn[rs, cs]) + bias
            yb_ref[0, rs, cs] = (u[rs, cs] * sz).astype(BF16)

    pa = _dg(xn, wa_ref[...])
    shift_ref[8:8 + tm, :] = pa
    shifted = shift_ref[7:7 + tm, :]
    pa_ref[0] = pa + (shifted - pa) * mu_ref[...]
    shift_ref[7:8, :] = pa[tm - 1:tm, :]


def _even_in_proj(h, norm_g, wa, wb, mu, bn_g, bn_b, sp_w, sp_bt, tm):
    bsz, seq, d = h.shape
    kern = functools.partial(_even_in_kernel, tm=tm)
    return pl.pallas_call(
        kern,
        grid=(bsz, seq // tm),
        in_specs=[
            pl.BlockSpec((1, tm, d), lambda b, s: (b, s, 0)),
            _const_spec((1, d)),
            _const_spec((d, P_A)),
            _const_spec((d, 2 * B_W)),
            _const_spec((1, P_A)),
            _const_spec((1, B_W)),
            _const_spec((1, B_W)),
            _const_spec((B_GROUPS, SP_CHUNK, SP_CHUNK)),
            _const_spec((SP_CHUNK, B_GROUPS)),
        ],
        out_specs=[
            pl.BlockSpec((1, tm, P_A), lambda b, s: (b, s, 0)),
            pl.BlockSpec((1, tm, B_W), lambda b, s: (b, s, 0)),
        ],
        out_shape=[
            jax.ShapeDtypeStruct((bsz, seq, P_A), F32),
            jax.ShapeDtypeStruct((bsz, seq, B_W), BF16),
        ],
        scratch_shapes=[pltpu.VMEM((tm + 8, P_A), F32)],
        compiler_params=_compiler_params(2, 48),
        name="even_in_proj",
    )(h, norm_g, wa, wb, mu, bn_g, bn_b, sp_w, sp_bt)


def _rwkv_kernel(pa_ref, w0_ref, a0_ref, kk_ref, ka_ref, rk_ref, lg_ref, lb_ref,
                 wup_ref, aup_ref, gup_ref, y_ref, s_ref, *, nch):
    C = SCAN_CHUNK
    tb = nch * C
    n_pairs = A_W // LANES

    @pl.when(pl.program_id(1) == 0)
    def _():
        s_ref[...] = jnp.zeros_like(s_ref)

    pa = pa_ref[0]
    r = pa[:, 0:A_W]
    k = pa[:, A_W:2 * A_W]
    v = pa[:, 2 * A_W:3 * A_W]
    wa_d = pa[:, 3 * A_W:3 * A_W + LANES]
    gd = pa[:, 3 * A_W + LANES:P_A]

    wlin = w0_ref[...] + _mmp(_split2(jnp.tanh(wa_d)), (wup_ref[0], wup_ref[1]))
    ld = -(2.718281828459045 ** -0.5) * _sigmoid(wlin)
    a = _sigmoid(a0_ref[...] + _dg(wa_d.astype(BF16), aup_ref[...]))
    gate = _dg(_sigmoid(gd).astype(BF16), gup_ref[...])

    row = lax.broadcasted_iota(jnp.int32, (C, LANES), 0)
    lane = lax.broadcasted_iota(jnp.int32, (C, LANES), 1)
    lane_in = jnp.bitwise_and(lane, HEAD_DIM - 1)
    m0 = lane < HEAD_DIM
    strict = lane_in < row
    incl = lane_in <= row
    eye_pair = jnp.where(lane_in == row, 1.0, 0.0)
    row2 = lax.broadcasted_iota(jnp.int32, (2 * C, LANES), 0)
    lane2 = lax.broadcasted_iota(jnp.int32, (2 * C, LANES), 1)
    mdiag = (row2 < C) == (lane2 < HEAD_DIM)
    mdiag_wide = jnp.concatenate([mdiag, mdiag], axis=1)
    eye2 = jnp.where(row2 == lane2, 1.0, 0.0)
    row4 = lax.broadcasted_iota(jnp.int32, (4 * C, LANES), 0)
    lane4 = lax.broadcasted_iota(jnp.int32, (4 * C, LANES), 1)
    mask4 = (jnp.bitwise_and(jnp.right_shift(row4, SCAN_CHUNK_LOG2), 1) == 0) == (lane4 < HEAD_DIM)
    zeros_c = jnp.zeros((C, LANES), F32)
    ones_pair = jnp.where(mdiag, 1.0, 0.0).astype(BF16)

    def head_sum(x):
        parts = _split2(x)
        stacked = jnp.concatenate(
            [q[:, p * LANES:(p + 1) * LANES] for q in parts for p in range(n_pairs)], axis=0)
        s = _dg(stacked, ones_pair)
        s = s[:n_pairs * tb] + s[n_pairs * tb:]
        return jnp.concatenate([s[p * tb:(p + 1) * tb] for p in range(n_pairs)], axis=1)

    kk = k * kk_ref[...]
    kkn = kk * jnp.minimum(lax.rsqrt(head_sum(kk * kk)), 1e12)
    k2 = k * (1.0 + (a - 1.0) * ka_ref[...])

    rc = lax.broadcasted_iota(jnp.int32, (tb, tb), 0)
    cc = lax.broadcasted_iota(jnp.int32, (tb, tb), 1)
    same_chunk = jnp.right_shift(rc, SCAN_CHUNK_LOG2) == jnp.right_shift(cc, SCAN_CHUNK_LOG2)
    lower_ones = jnp.where(jnp.logical_and(rc >= cc, same_chunk), 1.0, 0.0).astype(BF16)
    ldp = _split3(ld)
    cum = _dg(lower_ones, ldp[0]) + (_dg(lower_ones, ldp[1]) + _dg(lower_ones, ldp[2]))
    ecum = jnp.exp(cum)
    einv = jnp.exp(-cum)
    eprev = jnp.exp(cum - ld)
    rt = r * ecum
    at = -(kkn * eprev)
    bt = kkn * a * einv
    kt = k2 * einv

    def bd(x, n):
        return _parts(jnp.where(mdiag, jnp.concatenate([x, x], axis=0), 0.0), n)

    chains = [(j, p) for j in range(nch) for p in range(n_pairs)]

    def tile(x, jp):
        j, p = jp
        return x[j * C:(j + 1) * C, p * LANES:(p + 1) * LANES]

    at_c = [tile(at, jp) for jp in chains]
    rt_c = [tile(rt, jp) for jp in chains]
    bt_c = [tile(bt, jp) for jp in chains]
    kt_c = [tile(kt, jp) for jp in chains]
    v_c = [tile(v, jp) for jp in chains]

    nl, nr = _PARTS["gram"]
    gram = [
        _mmp(_parts(jnp.concatenate([at_c[i], rt_c[i]], axis=0), nl),
             _parts(jnp.where(mask4, jnp.concatenate([bt_c[i], bt_c[i], kt_c[i], kt_c[i]], axis=0), 0.0), nr),
             _NT)
        for i in range(len(chains))]
    a_ab = [jnp.where(strict, g[0:C, 0:LANES], 0.0) for g in gram]
    a_ak = [jnp.where(strict, g[0:C, LANES:], 0.0) for g in gram]
    m_rb = [jnp.where(incl, g[C:, 0:LANES], 0.0) for g in gram]
    m_rk = [jnp.where(incl, g[C:, LANES:], 0.0) for g in gram]

    nl, nr = _PARTS["akv"]
    akv = [_mmp(_parts(x, nl), bd(y, nr)) for x, y in zip(a_ak, v_c)]

    nl, nr = _PARTS["inv"]
    t_inv = [eye_pair + x for x in a_ab]
    pw = [_mmp(_parts(x, nl), bd(x, nr)) for x in a_ab]
    for _ in range(SCAN_CHUNK_LOG2 - 2):
        res = [_mmp(_parts(jnp.concatenate([t, w], axis=0), nl), bd(w, nr)) for t, w in zip(t_inv, pw)]
        t_inv = [t + x[0:C] for t, x in zip(t_inv, res)]
        pw = [x[C:] for x in res]
    t_inv = [t + _mmp(_parts(t, nl), bd(w, nr)) for t, w in zip(t_inv, pw)]

    nl, nr = _PARTS["pq"]
    pq = [
        _mmp(_parts(t, nl),
             _parts(jnp.where(mdiag_wide, jnp.concatenate([jnp.concatenate([x, y], axis=1)] * 2, axis=0), 0.0), nr))
        for t, x, y in zip(t_inv, at_c, akv)]
    nl, nr = _PARTS["gy"]
    gy = [
        _mmp(_parts(m, nl), _parts(jnp.where(mdiag_wide, jnp.concatenate([x, x], axis=0), 0.0), nr))
        for m, x in zip(m_rb, pq)]
    g_mat = [x + g[:, 0:LANES] for x, g in zip(rt_c, gy)]
    y_in = [g[:, LANES:] + _mmp(_parts(m, nl), bd(y, nr)) for g, m, y in zip(gy, m_rk, v_c)]

    nl, nr = _PARTS["trans"]
    phi_t, psi_t = [], []
    for i, (j, p) in enumerate(chains):
        pt0 = jnp.concatenate([pq[i][:, 0:LANES], zeros_c], axis=0).T
        qvt = jnp.concatenate([pq[i][:, LANES:], v_c[i]], axis=0).T
        bk = _parts(jnp.concatenate([bt_c[i], kt_c[i]], axis=0), nr)
        full = _mmp(_parts(jnp.concatenate([pt0, qvt], axis=0), nl), bk)
        w_end = ecum[(j + 1) * C - 1:(j + 1) * C, p * LANES:(p + 1) * LANES]
        phi_t.append(jnp.where(mdiag, eye2 + full[0:2 * C], 0.0) * w_end)
        psi_t.append(jnp.where(m0, full[2 * C:3 * C], full[3 * C:]) * w_end)

    state = [s_ref[p] for p in range(n_pairs)]
    ol, orr = _PARTS["out"]
    sl, sr = _PARTS["state"]
    y_rows = []
    for j in range(nch):
        idx = [j * n_pairs + p for p in range(n_pairs)]
        y_rows.append(jnp.concatenate(
            [_mmp(_parts(g_mat[i], ol), bd(state[p], orr), _NT) + y_in[i] for p, i in enumerate(idx)], axis=1))
        state = [_mmp(_parts(state[p], sl), _parts(phi_t[i], sr)) + psi_t[i] for p, i in enumerate(idx)]
    for p in range(n_pairs):
        s_ref[p] = state[p]

    y = jnp.concatenate(y_rows, axis=0)
    inv_n = 1.0 / HEAD_DIM
    yc = y - head_sum(y) * inv_n
    yv = head_sum(yc * yc) * inv_n
    yn = yc * lax.rsqrt(yv + LNX_EPS) * lg_ref[...] + lb_ref[...]
    bonus = head_sum(r * k2 * rk_ref[...]) * v
    y_ref[0] = ((yn + bonus) * gate).astype(BF16)


def _rwkv_mix(pa, w0, a0, k_k, k_a, r_k, lnx_g, lnx_b, wup, aup, gup):
    bsz, seq, _ = pa.shape
    nch = SCAN_CHUNKS_PER_STEP
    tb = nch * SCAN_CHUNK
    vec = _const_spec((1, A_W))
    lora = _const_spec((LANES, A_W))
    return pl.pallas_call(
        functools.partial(_rwkv_kernel, nch=nch),
        grid=(bsz, seq // tb),
        in_specs=[pl.BlockSpec((1, tb, P_A), lambda b, c: (b, c, 0)),
                  vec, vec, vec, vec, vec, vec, vec, _const_spec((2, LANES, A_W)), lora, lora],
        out_specs=pl.BlockSpec((1, tb, A_W), lambda b, c: (b, c, 0)),
        out_shape=jax.ShapeDtypeStruct((bsz, seq, A_W), BF16),
        scratch_shapes=[pltpu.VMEM((A_W // LANES, SCAN_CHUNK, LANES), F32)],
        compiler_params=_compiler_params(2, 32),
        name="rwkv7_mix",
    )(pa, w0, a0, k_k, k_a, r_k, lnx_g, lnx_b, wup, aup, gup)


def _odd_in_kernel(x_ref, g_ref, w_ref, cw_ref, cb_ref, cng_ref, cnb_ref, pw_ref, ps_ref,
                   yc_ref, yd_ref, gbuf, dbuf, pbuf, *, tm):
    s = pl.program_id(1)

    @pl.when(s == 0)
    def _():
        gbuf[0:CONV_HALO, :] = jnp.zeros((CONV_HALO, C_W), F32)
        dbuf[0:POOL_HALO, :] = jnp.zeros((POOL_HALO, D_W), F32)

    xn = _rms(x_ref[0], g_ref[...]).astype(BF16)
    proj = _dg(xn, w_ref[...])
    glu = proj[:, :C_W] * _sigmoid(proj[:, C_W:2 * C_W])
    d = proj[:, 2 * C_W:]
    gbuf[CONV_HALO:CONV_HALO + tm, :] = glu
    dbuf[POOL_HALO:POOL_HALO + tm, :] = d

    first = CONV_HALO - (CONV_WIDTH - 1)
    for rt in range(tm // CONV_ROWS):
        base = rt * CONV_ROWS
        acc = jnp.broadcast_to(cb_ref[...], (CONV_ROWS, C_W))
        for r in range(SUBLANES):
            rows = CONV_ROWS if r == 0 else CONV_ROWS + SUBLANES
            part = None
            for o in range(first, first + CONV_WIDTH):
                if o % SUBLANES != r:
                    continue
                lo = base + o - r
                term = gbuf[lo:lo + rows, :] * cw_ref[o - first:o - first + 1, :]
                part = term if part is None else part + term
            if r == 0:
                acc = acc + part
            else:
                pbuf[r, 0:rows, :] = part
                acc = acc + pbuf[r, r:r + CONV_ROWS, :]
        ln = _layer_norm(acc, cng_ref[...], cnb_ref[...], LN_EPS)
        yc_ref[0, base:base + CONV_ROWS, :] = (ln * _sigmoid(ln)).astype(BF16)

    t_idx = s * tm + lax.broadcasted_iota(jnp.int32, (tm, 1), 0)
    for gi, win in enumerate(POOL_WINDOWS):
        cs = slice(gi * POOL_GROUP_DIM, (gi + 1) * POOL_GROUP_DIM)
        tot = dbuf[POOL_HALO:POOL_HALO + tm, cs]
        for i in range(1, win):
            tot = tot + dbuf[POOL_HALO - i:POOL_HALO - i + tm, cs]
        inv_count = 1.0 / jnp.minimum(t_idx + 1, win).astype(F32)
        pooled = tot * inv_count - d[:, cs]
        yd = _dg(pooled.astype(BF16), pw_ref[gi].astype(BF16)) * ps_ref[:, cs]
        yd_ref[0, :, cs] = yd.astype(BF16)

    gbuf[0:CONV_HALO, :] = gbuf[tm:tm + CONV_HALO, :]
    dbuf[0:POOL_HALO, :] = dbuf[tm:tm + POOL_HALO, :]


def _odd_in_proj(h, norm_g, w_in, conv_w, conv_b, cn_g, cn_b, pool_w, pool_scale, tm):
    bsz, seq, d = h.shape
    kern = functools.partial(_odd_in_kernel, tm=tm)
    n_groups = len(POOL_WINDOWS)
    return pl.pallas_call(
        kern,
        grid=(bsz, seq // tm),
        in_specs=[
            pl.BlockSpec((1, tm, d), lambda b, s: (b, s, 0)),
            _const_spec((1, d)),
            _const_spec((d, P_ODD)),
            _const_spec((CONV_WIDTH, C_W)),
            _const_spec((1, C_W)),
            _const_spec((1, C_W)),
            _const_spec((1, C_W)),
            _const_spec((n_groups, POOL_GROUP_DIM, POOL_GROUP_DIM)),
            _const_spec((1, D_W)),
        ],
        out_specs=[
            pl.BlockSpec((1, tm, C_W), lambda b, s: (b, s, 0)),
            pl.BlockSpec((1, tm, D_W), lambda b, s: (b, s, 0)),
        ],
        out_shape=[
            jax.ShapeDtypeStruct((bsz, seq, C_W), BF16),
            jax.ShapeDtypeStruct((bsz, seq, D_W), BF16),
        ],
        scratch_shapes=[pltpu.VMEM((tm + CONV_HALO, C_W), F32),
                        pltpu.VMEM((tm + POOL_HALO, D_W), F32),
                        pltpu.VMEM((SUBLANES, CONV_ROWS + SUBLANES, C_W), F32)],
        compiler_params=_compiler_params(2, 40),
        name="odd_in_proj",
    )(h, norm_g, w_in, conv_w, conv_b, cn_g, cn_b, pool_w, pool_scale)


def _ffn_kernel(h_ref, ya_ref, yb_ref, woa_ref, wob_ref, fg_ref, wg_ref, wu_ref, wd_ref, fin_ref,
                o_ref, *, final):
    h1 = h_ref[...] + (_dg(ya_ref[...], woa_ref[...]) + _dg(yb_ref[...], wob_ref[...]))
    xn = _rms(h1, fg_ref[...]).astype(BF16)
    acc = jnp.zeros_like(h1)
    for j in range(0, FFN_HIDDEN, FFN_HIDDEN_CHUNK):
        gt = _dg(xn, wg_ref[:, j:j + FFN_HIDDEN_CHUNK])
        up = _dg(xn, wu_ref[:, j:j + FFN_HIDDEN_CHUNK])
        act = (gt * _sigmoid(gt) * up).astype(BF16)
        acc = acc + _dg(act, wd_ref[j:j + FFN_HIDDEN_CHUNK, :])
    out = h1 + acc
    if final:
        out = _rms(out, fin_ref[...])
    o_ref[...] = out


def _out_proj_ffn(h, ya, yb, wo_a, wo_b, ffn_g, w_gate, w_up, w_down, final_g, final, tm):
    t, d = h.shape
    half = ya.shape[1]
    kern = functools.partial(_ffn_kernel, final=final)
    return pl.pallas_call(
        kern,
        grid=(t // tm,),
        in_specs=[
            pl.BlockSpec((tm, d), lambda i: (i, 0)),
            pl.BlockSpec((tm, half), lambda i: (i, 0)),
            pl.BlockSpec((tm, half), lambda i: (i, 0)),
            _const_spec((half, d)),
            _const_spec((half, d)),
            _const_spec((1, d)),
            _const_spec((d, FFN_HIDDEN)),
            _const_spec((d, FFN_HIDDEN)),
            _const_spec((FFN_HIDDEN, d)),
            _const_spec((1, d)),
        ],
        out_specs=pl.BlockSpec((tm, d), lambda i: (i, 0)),
        out_shape=jax.ShapeDtypeStruct((t, d), F32),
        compiler_params=_compiler_params(1, 52),
        name="out_proj_ffn",
    )(h, ya, yb, wo_a, wo_b, ffn_g, w_gate, w_up, w_down, final_g)


def _hi_lo(w):
    hi = w.astype(BF16)
    lo = (w - hi.astype(F32)).astype(BF16)
    return jnp.stack([hi, lo])


def kernel(x, mix_norm_g, ffn_norm_g, final_norm_g, ev_w_in, ev_shift_mu, ev_w0, ev_w_up, ev_a0, ev_a_up, ev_g_up, ev_k_k, ev_k_a, ev_r_k, ev_lnx_g, ev_lnx_b, ev_bnorm_g, ev_bnorm_b, ev_spatial_w, ev_spatial_b, ev_w_out, od_w_in, od_conv_w, od_conv_b, od_cnorm_g, od_cnorm_b, od_pool_w, od_pool_scale, od_w_out, ffn_w_gate, ffn_w_up, ffn_w_down):
    bsz, seq, d = x.shape
    tm = min(512, seq)
    row = lambda vec: vec.reshape(1, -1)
    zpad = jnp.zeros((LANES - W_LORA, A_W), F32)

    h = x
    for layer in range(DEPTH):
        i = layer // 2
        if layer % 2 == 0:
            w_in = ev_w_in[i].astype(BF16)
            pa, y2 = _even_in_proj(
                h, row(mix_norm_g[layer]), w_in[:, :P_A], w_in[:, P_A:], row(ev_shift_mu[i]),
                row(ev_bnorm_g[i]), row(ev_bnorm_b[i]), ev_spatial_w[i], ev_spatial_b[i].T, tm)
            y1 = _rwkv_mix(
                pa, row(ev_w0[i]), row(ev_a0[i]), row(ev_k_k[i]), row(ev_k_a[i]), row(ev_r_k[i]),
                row(ev_lnx_g[i]), row(ev_lnx_b[i]),
                _hi_lo(jnp.concatenate([ev_w_up[i], zpad], axis=0)),
                jnp.concatenate([zpad, ev_a_up[i]], axis=0).astype(BF16),
                ev_g_up[i].astype(BF16))
            w_out = ev_w_out[i].astype(BF16)
        else:
            y1, y2 = _odd_in_proj(
                h, row(mix_norm_g[layer]), od_w_in[i].astype(BF16), od_conv_w[i], row(od_conv_b[i]),
                row(od_cnorm_g[i]), row(od_cnorm_b[i]), od_pool_w[i], row(od_pool_scale[i]), tm)
            w_out = od_w_out[i].astype(BF16)
        half = d // 2
        h = _out_proj_ffn(
            h.reshape(bsz * seq, d), y1.reshape(bsz * seq, half), y2.reshape(bsz * seq, half),
            w_out[:half], w_out[half:], row(ffn_norm_g[layer]),
            ffn_w_gate[layer].astype(BF16), ffn_w_up[layer].astype(BF16),
            ffn_w_down[layer].astype(BF16), row(final_norm_g),
            layer == DEPTH - 1, tm).reshape(bsz, seq, d)
    return h
```

```python
import functools

import jax
import jax.numpy as jnp
from jax import lax
from jax.experimental import pallas as pl
from jax.experimental.pallas import tpu as pltpu

F32 = jnp.float32
BF16 = jnp.bfloat16

D_MODEL = 1024
DEPTH = 4
A_W = 512
HEAD_DIM = 64
W_LORA = 64
A_LORA = 64
G_LORA = 128
P_A = 3 * A_W + W_LORA + A_LORA + G_LORA
LNX_EPS = 64e-5
B_W = 512
B_GROUPS = 4
SP_CHUNK = 128
C_W = 512
CONV_WIDTH = 31
D_W = 512
POOL_WINDOWS = (2, 4, 8, 16)
POOL_GROUP_DIM = 128
P_ODD = 2 * C_W + D_W
FFN_HIDDEN = 2816
RMS_EPS = 1e-5
LN_EPS = 1e-5

LANES = 128
SCAN_CHUNK = 64
SCAN_CHUNK_LOG2 = SCAN_CHUNK.bit_length() - 1
SCAN_CHUNKS_PER_STEP = 4
CONV_HALO = 32
POOL_HALO = 16
CONV_ROWS = 128
SUBLANES = 8
FFN_HIDDEN_CHUNK = 256
MIB = 1024 * 1024

_NN = (((1,), (0,)), ((), ()))
_NT = (((1,), (1,)), ((), ()))

_PARTS = {
    "gram": (1, 1),
    "inv": (1, 1),
    "akv": (1, 1),
    "pq": (1, 1),
    "gy": (1, 1),
    "trans": (1, 1),
    "out": (1, 1),
    "state": (2, 1),
}


def _dg(a, b, dims=_NN):
    return lax.dot_general(a, b, dims, preferred_element_type=F32)


def _split2(x):
    hi = x.astype(BF16)
    lo = (x - hi.astype(F32)).astype(BF16)
    return hi, lo


def _parts(x, n):
    return (x.astype(BF16),) if n == 1 else _split2(x)


def _mmp(a, b, dims=_NN):
    if len(a) == 2:
        m = a[0].shape[0]
        both = _dg(jnp.concatenate(a, axis=0), b[0], dims)
        out = both[:m] + both[m:]
    else:
        out = _dg(a[0], b[0], dims)
    if len(b) == 2:
        out = out + _dg(a[0], b[1], dims)
    return out


def _rms(x, g):
    return x * lax.rsqrt(jnp.mean(x * x, axis=-1, keepdims=True) + RMS_EPS) * g


def _layer_norm(x, g, b, eps):
    mu = jnp.mean(x, axis=-1, keepdims=True)
    xc = x - mu
    var = jnp.mean(xc * xc, axis=-1, keepdims=True)
    return xc * lax.rsqrt(var + eps) * g + b


def _sigmoid(x):
    return jax.nn.sigmoid(x)


def _compiler_params(n_grid, vmem_mib):
    return pltpu.CompilerParams(
        dimension_semantics=("arbitrary",) * n_grid,
        vmem_limit_bytes=vmem_mib * MIB,
    )


def _const_spec(shape):
    nd = len(shape)
    return pl.BlockSpec(shape, lambda *_: (0,) * nd, pipeline_mode=pl.Buffered(1))


def _even_in_kernel(x_ref, g_ref, wa_ref, wb_ref, mu_ref, bng_ref, bnb_ref, spw_ref, spbt_ref,
                    pa_ref, yb_ref, shift_ref, *, tm):
    @pl.when(pl.program_id(1) == 0)
    def _():
        shift_ref[0:8, :] = jnp.zeros((8, P_A), F32)

    xn = _rms(x_ref[0], g_ref[...]).astype(BF16)

    pb = _dg(xn, wb_ref[...])
    gb = 0.5 * pb * (1.0 + lax.erf(pb * (2.0 ** -0.5)))
    u = gb[:, :B_W]
    zn = _layer_norm(gb[:, B_W:], bng_ref[...], bnb_ref[...], LN_EPS).astype(BF16)
    row = lax.broadcasted_iota(jnp.int32, (SP_CHUNK, SP_CHUNK), 0)
    col = lax.broadcasted_iota(jnp.int32, (SP_CHUNK, SP_CHUNK), 1)
    tril = row >= col
    for g in range(B_GROUPS):
        wg = jnp.where(tril, spw_ref[g], 0.0).astype(BF16)
        bias = spbt_ref[:, g:g + 1]
        cs = slice(g * LANES, (g + 1) * LANES)
        for c in range(tm // SP_CHUNK):
            rs = slice(c * SP_CHUNK, (c + 1) * SP_CHUNK)
            sz = _dg(wg, zn[rs, cs]) + bias
            yb_ref[0, rs, cs] = (u[rs, cs] * sz).astype(BF16)

    pa = _dg(xn, wa_ref[...])
    shift_ref[8:8 + tm, :] = pa
    shifted = shift_ref[7:7 + tm, :]
    pa_ref[0] = pa + (shifted - pa) * mu_ref[...]
    shift_ref[7:8, :] = pa[tm - 1:tm, :]


def _even_in_proj(h, norm_g, wa, wb, mu, bn_g, bn_b, sp_w, sp_bt, tm):
    bsz, seq, d = h.shape
    kern = functools.partial(_even_in_kernel, tm=tm)
    return pl.pallas_call(
        kern,
        grid=(bsz, seq // tm),
        in_specs=[
            pl.BlockSpec((1, tm, d), lambda b, s: (b, s, 0)),
            _const_spec((1, d)),
            _const_spec((d, P_A)),
            _const_spec((d, 2 * B_W)),
            _const_spec((1, P_A)),
            _const_spec((1, B_W)),
            _const_spec((1, B_W)),
            _const_spec((B_GROUPS, SP_CHUNK, SP_CHUNK)),
            _const_spec((SP_CHUNK, B_GROUPS)),
        ],
        out_specs=[
            pl.BlockSpec((1, tm, P_A), lambda b, s: (b, s, 0)),
            pl.BlockSpec((1, tm, B_W), lambda b, s: (b, s, 0)),
        ],
        out_shape=[
            jax.ShapeDtypeStruct((bsz, seq, P_A), F32),
            jax.ShapeDtypeStruct((bsz, seq, B_W), BF16),
        ],
        scratch_shapes=[pltpu.VMEM((tm + 8, P_A), F32)],
        compiler_params=_compiler_params(2, 48),
        name="even_in_proj",
    )(h, norm_g, wa, wb, mu, bn_g, bn_b, sp_w, sp_bt)


def _rwkv_kernel(pa_ref, w0_ref, a0_ref, kk_ref, ka_ref, rk_ref, lg_ref, lb_ref,
                 wup_ref, aup_ref, gup_ref, y_ref, s_ref, *, nch):
    C = SCAN_CHUNK
    tb = nch * C
    n_pairs = A_W // LANES

    @pl.when(pl.program_id(1) == 0)
    def _():
        s_ref[...] = jnp.zeros_like(s_ref)

    pa = pa_ref[0]
    r = pa[:, 0:A_W]
    k = pa[:, A_W:2 * A_W]
    v = pa[:, 2 * A_W:3 * A_W]
    wa_d = pa[:, 3 * A_W:3 * A_W + LANES]
    gd = pa[:, 3 * A_W + LANES:P_A]

    wlin = w0_ref[...] + _mmp(_split2(jnp.tanh(wa_d)), (wup_ref[0], wup_ref[1]))
    ld = -(2.718281828459045 ** -0.5) * _sigmoid(wlin)
    a = _sigmoid(a0_ref[...] + _dg(wa_d.astype(BF16), aup_ref[...]))
    gate = _dg(_sigmoid(gd).astype(BF16), gup_ref[...])

    row = lax.broadcasted_iota(jnp.int32, (C, LANES), 0)
    lane = lax.broadcasted_iota(jnp.int32, (C, LANES), 1)
    lane_in = jnp.bitwise_and(lane, HEAD_DIM - 1)
    m0 = lane < HEAD_DIM
    strict = lane_in < row
    incl = lane_in <= row
    eye_pair = jnp.where(lane_in == row, 1.0, 0.0)
    row2 = lax.broadcasted_iota(jnp.int32, (2 * C, LANES), 0)
    lane2 = lax.broadcasted_iota(jnp.int32, (2 * C, LANES), 1)
    mdiag = (row2 < C) == (lane2 < HEAD_DIM)
    mdiag_wide = jnp.concatenate([mdiag, mdiag], axis=1)
    eye2 = jnp.where(row2 == lane2, 1.0, 0.0)
    row4 = lax.broadcasted_iota(jnp.int32, (4 * C, LANES), 0)
    lane4 = lax.broadcasted_iota(jnp.int32, (4 * C, LANES), 1)
    mask4 = (jnp.bitwise_and(jnp.right_shift(row4, SCAN_CHUNK_LOG2), 1) == 0) == (lane4 < HEAD_DIM)
    zeros_c = jnp.zeros((C, LANES), F32)
    ones_pair = jnp.where(mdiag, 1.0, 0.0).astype(BF16)

    def head_sum(x):
        xb = x.astype(BF16)
        stacked = jnp.concatenate([xb[:, p * LANES:(p + 1) * LANES] for p in range(n_pairs)], axis=0)
        s = _dg(stacked, ones_pair)
        return jnp.concatenate([s[p * tb:(p + 1) * tb] for p in range(n_pairs)], axis=1)

    kk = k * kk_ref[...]
    kkn = kk * jnp.minimum(lax.rsqrt(head_sum(kk * kk)), 1e12)
    k2 = k * (1.0 + (a - 1.0) * ka_ref[...])

    rc = lax.broadcasted_iota(jnp.int32, (tb, tb), 0)
    cc = lax.broadcasted_iota(jnp.int32, (tb, tb), 1)
    same_chunk = jnp.right_shift(rc, SCAN_CHUNK_LOG2) == jnp.right_shift(cc, SCAN_CHUNK_LOG2)
    lower_ones = jnp.where(jnp.logical_and(rc >= cc, same_chunk), 1.0, 0.0).astype(BF16)
    ldp = _split2(ld)
    cum = _dg(lower_ones, ldp[0]) + _dg(lower_ones, ldp[1])
    ecum = jnp.exp(cum)
    einv = jnp.exp(-cum)
    eprev = jnp.exp(cum - ld)
    rt = r * ecum
    at = -(kkn * eprev)
    bt = kkn * a * einv
    kt = k2 * einv

    def bd(x, n):
        return _parts(jnp.where(mdiag, jnp.concatenate([x, x], axis=0), 0.0), n)

    chains = [(j, p) for j in range(nch) for p in range(n_pairs)]

    def tile(x, jp):
        j, p = jp
        return x[j * C:(j + 1) * C, p * LANES:(p + 1) * LANES]

    at_c = [tile(at, jp) for jp in chains]
    rt_c = [tile(rt, jp) for jp in chains]
    bt_c = [tile(bt, jp) for jp in chains]
    kt_c = [tile(kt, jp) for jp in chains]
    v_c = [tile(v, jp) for jp in chains]

    nl, nr = _PARTS["gram"]
    gram = [
        _mmp(_parts(jnp.concatenate([at_c[i], rt_c[i]], axis=0), nl),
             _parts(jnp.where(mask4, jnp.concatenate([bt_c[i], bt_c[i], kt_c[i], kt_c[i]], axis=0), 0.0), nr),
             _NT)
        for i in range(len(chains))]
    a_ab = [jnp.where(strict, g[0:C, 0:LANES], 0.0) for g in gram]
    a_ak = [jnp.where(strict, g[0:C, LANES:], 0.0) for g in gram]
    m_rb = [jnp.where(incl, g[C:, 0:LANES], 0.0) for g in gram]
    m_rk = [jnp.where(incl, g[C:, LANES:], 0.0) for g in gram]

    nl, nr = _PARTS["akv"]
    akv = [_mmp(_parts(x, nl), bd(y, nr)) for x, y in zip(a_ak, v_c)]

    nl, nr = _PARTS["inv"]
    t_inv = [eye_pair + x for x in a_ab]
    pw = [_mmp(_parts(x, nl), bd(x, nr)) for x in a_ab]
    for _ in range(SCAN_CHUNK_LOG2 - 2):
        res = [_mmp(_parts(jnp.concatenate([t, w], axis=0), nl), bd(w, nr)) for t, w in zip(t_inv, pw)]
        t_inv = [t + x[0:C] for t, x in zip(t_inv, res)]
        pw = [x[C:] for x in res]
    t_inv = [t + _mmp(_parts(t, nl), bd(w, nr)) for t, w in zip(t_inv, pw)]

    nl, nr = _PARTS["pq"]
    pq = [
        _mmp(_parts(t, nl),
             _parts(jnp.where(mdiag_wide, jnp.concatenate([jnp.concatenate([x, y], axis=1)] * 2, axis=0), 0.0), nr))
        for t, x, y in zip(t_inv, at_c, akv)]
    nl, nr = _PARTS["gy"]
    gy = [
        _mmp(_parts(m, nl), _parts(jnp.where(mdiag_wide, jnp.concatenate([x, x], axis=0), 0.0), nr))
        for m, x in zip(m_rb, pq)]
    g_mat = [x + g[:, 0:LANES] for x, g in zip(rt_c, gy)]
    y_in = [g[:, LANES:] + _mmp(_parts(m, nl), bd(y, nr)) for g, m, y in zip(gy, m_rk, v_c)]

    nl, nr = _PARTS["trans"]
    phi_t, psi_t = [], []
    for i, (j, p) in enumerate(chains):
        pt0 = jnp.concatenate([pq[i][:, 0:LANES], zeros_c], axis=0).T
        qvt = jnp.concatenate([pq[i][:, LANES:], v_c[i]], axis=0).T
        bk = _parts(jnp.concatenate([bt_c[i], kt_c[i]], axis=0), nr)
        full = _mmp(_parts(jnp.concatenate([pt0, qvt], axis=0), nl), bk)
        w_end = ecum[(j + 1) * C - 1:(j + 1) * C, p * LANES:(p + 1) * LANES]
        phi_t.append(jnp.where(mdiag, eye2 + full[0:2 * C], 0.0) * w_end)
        psi_t.append(jnp.where(m0, full[2 * C:3 * C], full[3 * C:]) * w_end)

    state = [s_ref[p] for p in range(n_pairs)]
    ol, orr = _PARTS["out"]
    sl, sr = _PARTS["state"]
    y_rows = []
    for j in range(nch):
        idx = [j * n_pairs + p for p in range(n_pairs)]
        y_rows.append(jnp.concatenate(
            [_mmp(_parts(g_mat[i], ol), bd(state[p], orr), _NT) + y_in[i] for p, i in enumerate(idx)], axis=1))
        state = [_mmp(_parts(state[p], sl), _parts(phi_t[i], sr)) + psi_t[i] for p, i in enumerate(idx)]
    for p in range(n_pairs):
        s_ref[p] = state[p]

    y = jnp.concatenate(y_rows, axis=0)
    inv_n = 1.0 / HEAD_DIM
    yc = y - head_sum(y) * inv_n
    yv = head_sum(yc * yc) * inv_n
    yn = yc * lax.rsqrt(yv + LNX_EPS) * lg_ref[...] + lb_ref[...]
    bonus = head_sum(r * k2 * rk_ref[...]) * v
    y_ref[0] = ((yn + bonus) * gate).astype(BF16)


def _rwkv_mix(pa, w0, a0, k_k, k_a, r_k, lnx_g, lnx_b, wup, aup, gup):
    bsz, seq, _ = pa.shape
    nch = SCAN_CHUNKS_PER_STEP
    tb = nch * SCAN_CHUNK
    vec = _const_spec((1, A_W))
    lora = _const_spec((LANES, A_W))
    return pl.pallas_call(
        functools.partial(_rwkv_kernel, nch=nch),
        grid=(bsz, seq // tb),
        in_specs=[pl.BlockSpec((1, tb, P_A), lambda b, c: (b, c, 0)),
                  vec, vec, vec, vec, vec, vec, vec, _const_spec((2, LANES, A_W)), lora, lora],
        out_specs=pl.BlockSpec((1, tb, A_W), lambda b, c: (b, c, 0)),
        out_shape=jax.ShapeDtypeStruct((bsz, seq, A_W), BF16),
        scratch_shapes=[pltpu.VMEM((A_W // LANES, SCAN_CHUNK, LANES), F32)],
        compiler_params=_compiler_params(2, 32),
        name="rwkv7_mix",
    )(pa, w0, a0, k_k, k_a, r_k, lnx_g, lnx_b, wup, aup, gup)


def _odd_in_kernel(x_ref, g_ref, w_ref, cw_ref, cb_ref, cng_ref, cnb_ref, pw_ref, ps_ref,
                   yc_ref, yd_ref, gbuf, dbuf, pbuf, *, tm):
    s = pl.program_id(1)

    @pl.when(s == 0)
    def _():
        gbuf[0:CONV_HALO, :] = jnp.zeros((CONV_HALO, C_W), F32)
        dbuf[0:POOL_HALO, :] = jnp.zeros((POOL_HALO, D_W), F32)

    xn = _rms(x_ref[0], g_ref[...]).astype(BF16)
    proj = _dg(xn, w_ref[...])
    glu = proj[:, :C_W] * _sigmoid(proj[:, C_W:2 * C_W])
    d = proj[:, 2 * C_W:]
    gbuf[CONV_HALO:CONV_HALO + tm, :] = glu
    dbuf[POOL_HALO:POOL_HALO + tm, :] = d

    first = CONV_HALO - (CONV_WIDTH - 1)
    for rt in range(tm // CONV_ROWS):
        base = rt * CONV_ROWS
        acc = jnp.broadcast_to(cb_ref[...], (CONV_ROWS, C_W))
        for r in range(SUBLANES):
            rows = CONV_ROWS if r == 0 else CONV_ROWS + SUBLANES
            part = None
            for o in range(first, first + CONV_WIDTH):
                if o % SUBLANES != r:
                    continue
                lo = base + o - r
                term = gbuf[lo:lo + rows, :] * cw_ref[o - first:o - first + 1, :]
                part = term if part is None else part + term
            if r == 0:
                acc = acc + part
            else:
                pbuf[r, 0:rows, :] = part
                acc = acc + pbuf[r, r:r + CONV_ROWS, :]
        ln = _layer_norm(acc, cng_ref[...], cnb_ref[...], LN_EPS)
        yc_ref[0, base:base + CONV_ROWS, :] = (ln * _sigmoid(ln)).astype(BF16)

    t_idx = s * tm + lax.broadcasted_iota(jnp.int32, (tm, 1), 0)
    for gi, win in enumerate(POOL_WINDOWS):
        cs = slice(gi * POOL_GROUP_DIM, (gi + 1) * POOL_GROUP_DIM)
        tot = dbuf[POOL_HALO:POOL_HALO + tm, cs]
        for i in range(1, win):
            tot = tot + dbuf[POOL_HALO - i:POOL_HALO - i + tm, cs]
        inv_count = 1.0 / jnp.minimum(t_idx + 1, win).astype(F32)
        pooled = tot * inv_count - d[:, cs]
        yd = _dg(pooled.astype(BF16), pw_ref[gi].astype(BF16)) * ps_ref[:, cs]
        yd_ref[0, :, cs] = yd.astype(BF16)

    gbuf[0:CONV_HALO, :] = gbuf[tm:tm + CONV_HALO, :]
    dbuf[0:POOL_HALO, :] = dbuf[tm:tm + POOL_HALO, :]


def _odd_in_proj(h, norm_g, w_in, conv_w, conv_b, cn_g, cn_b, pool_w, pool_scale, tm):
    bsz, seq, d = h.shape
    kern = functools.partial(_odd_in_kernel, tm=tm)
    n_groups = len(POOL_WINDOWS)
    return pl.pallas_call(
        kern,
        grid=(bsz, seq // tm),
        in_specs=[
            pl.BlockSpec((1, tm, d), lambda b, s: (b, s, 0)),
            _const_spec((1, d)),
            _const_spec((d, P_ODD)),
            _const_spec((CONV_WIDTH, C_W)),
            _const_spec((1, C_W)),
            _const_spec((1, C_W)),
            _const_spec((1, C_W)),
            _const_spec((n_groups, POOL_GROUP_DIM, POOL_GROUP_DIM)),
            _const_spec((1, D_W)),
        ],
        out_specs=[
            pl.BlockSpec((1, tm, C_W), lambda b, s: (b, s, 0)),
            pl.BlockSpec((1, tm, D_W), lambda b, s: (b, s, 0)),
        ],
        out_shape=[
            jax.ShapeDtypeStruct((bsz, seq, C_W), BF16),
            jax.ShapeDtypeStruct((bsz, seq, D_W), BF16),
        ],
        scratch_shapes=[pltpu.VMEM((tm + CONV_HALO, C_W), F32),
                        pltpu.VMEM((tm + POOL_HALO, D_W), F32),
                        pltpu.VMEM((SUBLANES, CONV_ROWS + SUBLANES, C_W), F32)],
        compiler_params=_compiler_params(2, 40),
        name="odd_in_proj",
    )(h, norm_g, w_in, conv_w, conv_b, cn_g, cn_b, pool_w, pool_scale)


def _ffn_kernel(h_ref, ya_ref, yb_ref, woa_ref, wob_ref, fg_ref, wg_ref, wu_ref, wd_ref, fin_ref,
                o_ref, *, final):
    h1 = h_ref[...] + (_dg(ya_ref[...], woa_ref[...]) + _dg(yb_ref[...], wob_ref[...]))
    xn = _rms(h1, fg_ref[...]).astype(BF16)
    acc = jnp.zeros_like(h1)
    for j in range(0, FFN_HIDDEN, FFN_HIDDEN_CHUNK):
        gt = _dg(xn, wg_ref[:, j:j + FFN_HIDDEN_CHUNK])
        up = _dg(xn, wu_ref[:, j:j + FFN_HIDDEN_CHUNK])
        act = (gt * _sigmoid(gt) * up).astype(BF16)
        acc = acc + _dg(act, wd_ref[j:j + FFN_HIDDEN_CHUNK, :])
    out = h1 + acc
    if final:
        out = _rms(out, fin_ref[...])
    o_ref[...] = out


def _out_proj_ffn(h, ya, yb, wo_a, wo_b, ffn_g, w_gate, w_up, w_down, final_g, final, tm):
    t, d = h.shape
    half = ya.shape[1]
    kern = functools.partial(_ffn_kernel, final=final)
    return pl.pallas_call(
        kern,
        grid=(t // tm,),
        in_specs=[
            pl.BlockSpec((tm, d), lambda i: (i, 0)),
            pl.BlockSpec((tm, half), lambda i: (i, 0)),
            pl.BlockSpec((tm, half), lambda i: (i, 0)),
            _const_spec((half, d)),
            _const_spec((half, d)),
            _const_spec((1, d)),
            _const_spec((d, FFN_HIDDEN)),
            _const_spec((d, FFN_HIDDEN)),
            _const_spec((FFN_HIDDEN, d)),
            _const_spec((1, d)),
        ],
        out_specs=pl.BlockSpec((tm, d), lambda i: (i, 0)),
        out_shape=jax.ShapeDtypeStruct((t, d), F32),
        compiler_params=_compiler_params(1, 52),
        name="out_proj_ffn",
    )(h, ya, yb, wo_a, wo_b, ffn_g, w_gate, w_up, w_down, final_g)


def _hi_lo(w):
    hi = w.astype(BF16)
    lo = (w - hi.astype(F32)).astype(BF16)
    return jnp.stack([hi, lo])


def kernel(x, mix_norm_g, ffn_norm_g, final_norm_g, ev_w_in, ev_shift_mu, ev_w0, ev_w_up, ev_a0, ev_a_up, ev_g_up, ev_k_k, ev_k_a, ev_r_k, ev_lnx_g, ev_lnx_b, ev_bnorm_g, ev_bnorm_b, ev_spatial_w, ev_spatial_b, ev_w_out, od_w_in, od_conv_w, od_conv_b, od_cnorm_g, od_cnorm_b, od_pool_w, od_pool_scale, od_w_out, ffn_w_gate, ffn_w_up, ffn_w_down):
    bsz, seq, d = x.shape
    tm = min(512, seq)
    row = lambda vec: vec.reshape(1, -1)
    zpad = jnp.zeros((LANES - W_LORA, A_W), F32)

    h = x
    for layer in range(DEPTH):
        i = layer // 2
        if layer % 2 == 0:
            w_in = ev_w_in[i].astype(BF16)
            pa, y2 = _even_in_proj(
                h, row(mix_norm_g[layer]), w_in[:, :P_A], w_in[:, P_A:], row(ev_shift_mu[i]),
                row(ev_bnorm_g[i]), row(ev_bnorm_b[i]), ev_spatial_w[i], ev_spatial_b[i].T, tm)
            y1 = _rwkv_mix(
                pa, row(ev_w0[i]), row(ev_a0[i]), row(ev_k_k[i]), row(ev_k_a[i]), row(ev_r_k[i]),
                row(ev_lnx_g[i]), row(ev_lnx_b[i]),
                _hi_lo(jnp.concatenate([ev_w_up[i], zpad], axis=0)),
                jnp.concatenate([zpad, ev_a_up[i]], axis=0).astype(BF16),
                ev_g_up[i].astype(BF16))
            w_out = ev_w_out[i].astype(BF16)
        else:
            y1, y2 = _odd_in_proj(
                h, row(mix_norm_g[layer]), od_w_in[i].astype(BF16), od_conv_w[i], row(od_conv_b[i]),
                row(od_cnorm_g[i]), row(od_cnorm_b[i]), od_pool_w[i], row(od_pool_scale[i]), tm)
            w_out = od_w_out[i].astype(BF16)
        half = d // 2
        h = _out_proj_ffn(
            h.reshape(bsz * seq, d), y1.reshape(bsz * seq, half), y2.reshape(bsz * seq, half),
            w_out[:half], w_out[half:], row(ffn_norm_g[layer]),
            ffn_w_gate[layer].astype(BF16), ffn_w_up[layer].astype(BF16),
            ffn_w_down[layer].astype(BF16), row(final_norm_g),
            layer == DEPTH - 1, tm).reshape(bsz, seq, d)
    return h
```

```python
import functools

import jax
import jax.numpy as jnp
from jax import lax
from jax.experimental import pallas as pl
from jax.experimental.pallas import tpu as pltpu

F32 = jnp.float32
BF16 = jnp.bfloat16

D_MODEL = 1024
DEPTH = 4
A_W = 512
HEAD_DIM = 64
W_LORA = 64
A_LORA = 64
G_LORA = 128
P_A = 3 * A_W + W_LORA + A_LORA + G_LORA
LNX_EPS = 64e-5
B_W = 512
B_GROUPS = 4
SP_CHUNK = 128
C_W = 512
CONV_WIDTH = 31
D_W = 512
POOL_WINDOWS = (2, 4, 8, 16)
POOL_GROUP_DIM = 128
P_ODD = 2 * C_W + D_W
FFN_HIDDEN = 2816
RMS_EPS = 1e-5
LN_EPS = 1e-5

LANES = 128
SCAN_CHUNK = 64
SCAN_CHUNK_LOG2 = SCAN_CHUNK.bit_length() - 1
SCAN_CHUNKS_PER_STEP = 4
CONV_HALO = 32
POOL_HALO = 16
CONV_ROWS = 128
SUBLANES = 8
FFN_HIDDEN_CHUNK = 256
MIB = 1024 * 1024

_NN = (((1,), (0,)), ((), ()))
_NT = (((1,), (1,)), ((), ()))

_PARTS = {
    "gram": (1, 1),
    "inv": (1, 1),
    "akv": (1, 1),
    "pq": (1, 1),
    "gy": (1, 1),
    "trans": (1, 1),
    "out": (1, 1),
    "state": (2, 1),
}


def _dg(a, b, dims=_NN):
    return lax.dot_general(a, b, dims, preferred_element_type=F32)


def _split2(x):
    hi = x.astype(BF16)
    lo = (x - hi.astype(F32)).astype(BF16)
    return hi, lo


def _parts(x, n):
    return (x.astype(BF16),) if n == 1 else _split2(x)


def _mmp(a, b, dims=_NN):
    if len(a) == 2:
        m = a[0].shape[0]
        both = _dg(jnp.concatenate(a, axis=0), b[0], dims)
        out = both[:m] + both[m:]
    else:
        out = _dg(a[0], b[0], dims)
    if len(b) == 2:
        out = out + _dg(a[0], b[1], dims)
    return out


def _rms(x, g):
    return x * lax.rsqrt(jnp.mean(x * x, axis=-1, keepdims=True) + RMS_EPS) * g


def _layer_norm(x, g, b, eps):
    mu = jnp.mean(x, axis=-1, keepdims=True)
    xc = x - mu
    var = jnp.mean(xc * xc, axis=-1, keepdims=True)
    return xc * lax.rsqrt(var + eps) * g + b


def _sigmoid(x):
    return jax.nn.sigmoid(x)


def _compiler_params(n_grid, vmem_mib):
    return pltpu.CompilerParams(
        dimension_semantics=("arbitrary",) * n_grid,
        vmem_limit_bytes=vmem_mib * MIB,
    )


def _const_spec(shape):
    nd = len(shape)
    return pl.BlockSpec(shape, lambda *_: (0,) * nd, pipeline_mode=pl.Buffered(1))


def _even_in_kernel(x_ref, g_ref, wa_ref, wb_ref, mu_ref, bng_ref, bnb_ref, spw_ref, spbt_ref,
                    pa_ref, yb_ref, shift_ref, *, tm):
    @pl.when(pl.program_id(1) == 0)
    def _():
        shift_ref[0:8, :] = jnp.zeros((8, P_A), F32)

    xn = _rms(x_ref[0], g_ref[...]).astype(BF16)

    pb = _dg(xn, wb_ref[...])
    gb = 0.5 * pb * (1.0 + lax.erf(pb * (2.0 ** -0.5)))
    u = gb[:, :B_W]
    zn = _layer_norm(gb[:, B_W:], bng_ref[...], bnb_ref[...], LN_EPS).astype(BF16)
    row = lax.broadcasted_iota(jnp.int32, (SP_CHUNK, SP_CHUNK), 0)
    col = lax.broadcasted_iota(jnp.int32, (SP_CHUNK, SP_CHUNK), 1)
    tril = row >= col
    sz = {}
    for g in range(B_GROUPS):
        wg = jnp.where(tril, spw_ref[g], 0.0).astype(BF16)
        for c in range(tm // SP_CHUNK):
            sz[g, c] = _dg(wg, zn[c * SP_CHUNK:(c + 1) * SP_CHUNK, g * LANES:(g + 1) * LANES])

    pa = _dg(xn, wa_ref[...])

    for g in range(B_GROUPS):
        bias = spbt_ref[:, g:g + 1]
        cs = slice(g * LANES, (g + 1) * LANES)
        for c in range(tm // SP_CHUNK):
            rs = slice(c * SP_CHUNK, (c + 1) * SP_CHUNK)
            yb_ref[0, rs, cs] = (u[rs, cs] * (sz[g, c] + bias)).astype(BF16)

    shift_ref[8:8 + tm, :] = pa
    shifted = shift_ref[7:7 + tm, :]
    pa_ref[0] = pa + (shifted - pa) * mu_ref[...]
    shift_ref[7:8, :] = pa[tm - 1:tm, :]


def _even_in_proj(h, norm_g, wa, wb, mu, bn_g, bn_b, sp_w, sp_bt, tm):
    bsz, seq, d = h.shape
    kern = functools.partial(_even_in_kernel, tm=tm)
    return pl.pallas_call(
        kern,
        grid=(bsz, seq // tm),
        in_specs=[
            pl.BlockSpec((1, tm, d), lambda b, s: (b, s, 0)),
            _const_spec((1, d)),
            _const_spec((d, P_A)),
            _const_spec((d, 2 * B_W)),
            _const_spec((1, P_A)),
            _const_spec((1, B_W)),
            _const_spec((1, B_W)),
            _const_spec((B_GROUPS, SP_CHUNK, SP_CHUNK)),
            _const_spec((SP_CHUNK, B_GROUPS)),
        ],
        out_specs=[
            pl.BlockSpec((1, tm, P_A), lambda b, s: (b, s, 0)),
            pl.BlockSpec((1, tm, B_W), lambda b, s: (b, s, 0)),
        ],
        out_shape=[
            jax.ShapeDtypeStruct((bsz, seq, P_A), F32),
            jax.ShapeDtypeStruct((bsz, seq, B_W), BF16),
        ],
        scratch_shapes=[pltpu.VMEM((tm + 8, P_A), F32)],
        compiler_params=_compiler_params(2, 48),
        name="even_in_proj",
    )(h, norm_g, wa, wb, mu, bn_g, bn_b, sp_w, sp_bt)


def _rwkv_kernel(pa_ref, w0_ref, a0_ref, kk_ref, ka_ref, rk_ref, lg_ref, lb_ref,
                 wup_ref, aup_ref, gup_ref, y_ref, s_ref, *, nch):
    C = SCAN_CHUNK
    tb = nch * C
    n_pairs = A_W // LANES

    @pl.when(pl.program_id(1) == 0)
    def _():
        s_ref[...] = jnp.zeros_like(s_ref)

    pa = pa_ref[0]
    r = pa[:, 0:A_W]
    k = pa[:, A_W:2 * A_W]
    v = pa[:, 2 * A_W:3 * A_W]
    wa_d = pa[:, 3 * A_W:3 * A_W + LANES]
    gd = pa[:, 3 * A_W + LANES:P_A]

    wlin = w0_ref[...] + _mmp(_split2(jnp.tanh(wa_d)), (wup_ref[0], wup_ref[1]))
    ld = -(2.718281828459045 ** -0.5) * _sigmoid(wlin)
    a = _sigmoid(a0_ref[...] + _dg(wa_d.astype(BF16), aup_ref[...]))
    gate = _dg(_sigmoid(gd).astype(BF16), gup_ref[...])

    row = lax.broadcasted_iota(jnp.int32, (C, LANES), 0)
    lane = lax.broadcasted_iota(jnp.int32, (C, LANES), 1)
    lane_in = jnp.bitwise_and(lane, HEAD_DIM - 1)
    m0 = lane < HEAD_DIM
    strict = lane_in < row
    incl = lane_in <= row
    eye_pair = jnp.where(lane_in == row, 1.0, 0.0)
    row2 = lax.broadcasted_iota(jnp.int32, (2 * C, LANES), 0)
    lane2 = lax.broadcasted_iota(jnp.int32, (2 * C, LANES), 1)
    mdiag = (row2 < C) == (lane2 < HEAD_DIM)
    mdiag_wide = jnp.concatenate([mdiag, mdiag], axis=1)
    eye2 = jnp.where(row2 == lane2, 1.0, 0.0)
    row4 = lax.broadcasted_iota(jnp.int32, (4 * C, LANES), 0)
    lane4 = lax.broadcasted_iota(jnp.int32, (4 * C, LANES), 1)
    mask4 = (jnp.bitwise_and(jnp.right_shift(row4, SCAN_CHUNK_LOG2), 1) == 0) == (lane4 < HEAD_DIM)
    zeros_c = jnp.zeros((C, LANES), F32)
    ones_pair = jnp.where(mdiag, 1.0, 0.0).astype(BF16)

    def head_sum(x):
        xb = x.astype(BF16)
        stacked = jnp.concatenate([xb[:, p * LANES:(p + 1) * LANES] for p in range(n_pairs)], axis=0)
        s = _dg(stacked, ones_pair)
        return jnp.concatenate([s[p * tb:(p + 1) * tb] for p in range(n_pairs)], axis=1)

    kk = k * kk_ref[...]
    kkn = kk * jnp.minimum(lax.rsqrt(head_sum(kk * kk)), 1e12)
    k2 = k * (1.0 + (a - 1.0) * ka_ref[...])

    rc = lax.broadcasted_iota(jnp.int32, (tb, tb), 0)
    cc = lax.broadcasted_iota(jnp.int32, (tb, tb), 1)
    same_chunk = jnp.right_shift(rc, SCAN_CHUNK_LOG2) == jnp.right_shift(cc, SCAN_CHUNK_LOG2)
    lower_ones = jnp.where(jnp.logical_and(rc >= cc, same_chunk), 1.0, 0.0).astype(BF16)
    ldp = _split2(ld)
    cum = _dg(lower_ones, ldp[0]) + _dg(lower_ones, ldp[1])
    ecum = jnp.exp(cum)
    einv = jnp.exp(-cum)
    eprev = jnp.exp(cum - ld)
    rt = r * ecum
    at = -(kkn * eprev)
    bt = kkn * a * einv
    kt = k2 * einv

    def bd(x, n):
        return _parts(jnp.where(mdiag, jnp.concatenate([x, x], axis=0), 0.0), n)

    chains = [(j, p) for j in range(nch) for p in range(n_pairs)]

    def tile(x, jp):
        j, p = jp
        return x[j * C:(j + 1) * C, p * LANES:(p + 1) * LANES]

    at_c = [tile(at, jp) for jp in chains]
    rt_c = [tile(rt, jp) for jp in chains]
    bt_c = [tile(bt, jp) for jp in chains]
    kt_c = [tile(kt, jp) for jp in chains]
    v_c = [tile(v, jp) for jp in chains]

    nl, nr = _PARTS["gram"]
    gram = [
        _mmp(_parts(jnp.concatenate([at_c[i], rt_c[i]], axis=0), nl),
             _parts(jnp.where(mask4, jnp.concatenate([bt_c[i], bt_c[i], kt_c[i], kt_c[i]], axis=0), 0.0), nr),
             _NT)
        for i in range(len(chains))]
    a_ab = [jnp.where(strict, g[0:C, 0:LANES], 0.0) for g in gram]
    a_ak = [jnp.where(strict, g[0:C, LANES:], 0.0) for g in gram]
    m_rb = [jnp.where(incl, g[C:, 0:LANES], 0.0) for g in gram]
    m_rk = [jnp.where(incl, g[C:, LANES:], 0.0) for g in gram]

    nl, nr = _PARTS["akv"]
    akv = [_mmp(_parts(x, nl), bd(y, nr)) for x, y in zip(a_ak, v_c)]

    nl, nr = _PARTS["inv"]
    t_inv = [eye_pair + x for x in a_ab]
    pw = [_mmp(_parts(x, nl), bd(x, nr)) for x in a_ab]
    for _ in range(SCAN_CHUNK_LOG2 - 2):
        res = [_mmp(_parts(jnp.concatenate([t, w], axis=0), nl), bd(w, nr)) for t, w in zip(t_inv, pw)]
        t_inv = [t + x[0:C] for t, x in zip(t_inv, res)]
        pw = [x[C:] for x in res]
    t_inv = [t + _mmp(_parts(t, nl), bd(w, nr)) for t, w in zip(t_inv, pw)]

    nl, nr = _PARTS["pq"]
    pq = [
        _mmp(_parts(t, nl),
             _parts(jnp.where(mdiag_wide, jnp.concatenate([jnp.concatenate([x, y], axis=1)] * 2, axis=0), 0.0), nr))
        for t, x, y in zip(t_inv, at_c, akv)]
    nl, nr = _PARTS["gy"]
    gy = [
        _mmp(_parts(m, nl), _parts(jnp.where(mdiag_wide, jnp.concatenate([x, x], axis=0), 0.0), nr))
        for m, x in zip(m_rb, pq)]
    g_mat = [x + g[:, 0:LANES] for x, g in zip(rt_c, gy)]
    y_in = [g[:, LANES:] + _mmp(_parts(m, nl), bd(y, nr)) for g, m, y in zip(gy, m_rk, v_c)]

    nl, nr = _PARTS["trans"]
    phi_t, psi_t = [], []
    for i, (j, p) in enumerate(chains):
        pt0 = jnp.concatenate([pq[i][:, 0:LANES], zeros_c], axis=0).T
        qvt = jnp.concatenate([pq[i][:, LANES:], v_c[i]], axis=0).T
        bk = _parts(jnp.concatenate([bt_c[i], kt_c[i]], axis=0), nr)
        full = _mmp(_parts(jnp.concatenate([pt0, qvt], axis=0), nl), bk)
        w_end = ecum[(j + 1) * C - 1:(j + 1) * C, p * LANES:(p + 1) * LANES]
        phi_t.append(jnp.where(mdiag, eye2 + full[0:2 * C], 0.0) * w_end)
        psi_t.append(jnp.where(m0, full[2 * C:3 * C], full[3 * C:]) * w_end)

    state = [s_ref[p] for p in range(n_pairs)]
    ol, orr = _PARTS["out"]
    sl, sr = _PARTS["state"]
    y_rows = []
    for j in range(nch):
        idx = [j * n_pairs + p for p in range(n_pairs)]
        y_rows.append(jnp.concatenate(
            [_mmp(_parts(g_mat[i], ol), bd(state[p], orr), _NT) + y_in[i] for p, i in enumerate(idx)], axis=1))
        state = [_mmp(_parts(state[p], sl), _parts(phi_t[i], sr)) + psi_t[i] for p, i in enumerate(idx)]
    for p in range(n_pairs):
        s_ref[p] = state[p]

    y = jnp.concatenate(y_rows, axis=0)
    inv_n = 1.0 / HEAD_DIM
    yc = y - head_sum(y) * inv_n
    yv = head_sum(yc * yc) * inv_n
    yn = yc * lax.rsqrt(yv + LNX_EPS) * lg_ref[...] + lb_ref[...]
    bonus = head_sum(r * k2 * rk_ref[...]) * v
    y_ref[0] = ((yn + bonus) * gate).astype(BF16)


def _rwkv_mix(pa, w0, a0, k_k, k_a, r_k, lnx_g, lnx_b, wup, aup, gup):
    bsz, seq, _ = pa.shape
    nch = SCAN_CHUNKS_PER_STEP
    tb = nch * SCAN_CHUNK
    vec = _const_spec((1, A_W))
    lora = _const_spec((LANES, A_W))
    return pl.pallas_call(
        functools.partial(_rwkv_kernel, nch=nch),
        grid=(bsz, seq // tb),
        in_specs=[pl.BlockSpec((1, tb, P_A), lambda b, c: (b, c, 0)),
                  vec, vec, vec, vec, vec, vec, vec, _const_spec((2, LANES, A_W)), lora, lora],
        out_specs=pl.BlockSpec((1, tb, A_W), lambda b, c: (b, c, 0)),
        out_shape=jax.ShapeDtypeStruct((bsz, seq, A_W), BF16),
        scratch_shapes=[pltpu.VMEM((A_W // LANES, SCAN_CHUNK, LANES), F32)],
        compiler_params=_compiler_params(2, 32),
        name="rwkv7_mix",
    )(pa, w0, a0, k_k, k_a, r_k, lnx_g, lnx_b, wup, aup, gup)


def _odd_in_kernel(x_ref, g_ref, w_ref, cw_ref, cb_ref, cng_ref, cnb_ref, pw_ref, ps_ref,
                   yc_ref, yd_ref, gbuf, dbuf, pbuf, *, tm):
    s = pl.program_id(1)

    @pl.when(s == 0)
    def _():
        gbuf[0:CONV_HALO, :] = jnp.zeros((CONV_HALO, C_W), F32)
        dbuf[0:POOL_HALO, :] = jnp.zeros((POOL_HALO, D_W), F32)

    xn = _rms(x_ref[0], g_ref[...]).astype(BF16)
    proj = _dg(xn, w_ref[...])
    glu = proj[:, :C_W] * _sigmoid(proj[:, C_W:2 * C_W])
    d = proj[:, 2 * C_W:]
    gbuf[CONV_HALO:CONV_HALO + tm, :] = glu
    dbuf[POOL_HALO:POOL_HALO + tm, :] = d

    first = CONV_HALO - (CONV_WIDTH - 1)
    for rt in range(tm // CONV_ROWS):
        base = rt * CONV_ROWS
        acc = jnp.broadcast_to(cb_ref[...], (CONV_ROWS, C_W))
        for r in range(SUBLANES):
            rows = CONV_ROWS if r == 0 else CONV_ROWS + SUBLANES
            part = None
            for o in range(first, first + CONV_WIDTH):
                if o % SUBLANES != r:
                    continue
                lo = base + o - r
                term = gbuf[lo:lo + rows, :] * cw_ref[o - first:o - first + 1, :]
                part = term if part is None else part + term
            if r == 0:
                acc = acc + part
            else:
                pbuf[r, 0:rows, :] = part
                acc = acc + pbuf[r, r:r + CONV_ROWS, :]
        ln = _layer_norm(acc, cng_ref[...], cnb_ref[...], LN_EPS)
        yc_ref[0, base:base + CONV_ROWS, :] = (ln * _sigmoid(ln)).astype(BF16)

    t_idx = s * tm + lax.broadcasted_iota(jnp.int32, (tm, 1), 0)
    for gi, win in enumerate(POOL_WINDOWS):
        cs = slice(gi * POOL_GROUP_DIM, (gi + 1) * POOL_GROUP_DIM)
        tot = dbuf[POOL_HALO:POOL_HALO + tm, cs]
        for i in range(1, win):
            tot = tot + dbuf[POOL_HALO - i:POOL_HALO - i + tm, cs]
        inv_count = 1.0 / jnp.minimum(t_idx + 1, win).astype(F32)
        pooled = tot * inv_count - d[:, cs]
        yd = _dg(pooled.astype(BF16), pw_ref[gi].astype(BF16)) * ps_ref[:, cs]
        yd_ref[0, :, cs] = yd.astype(BF16)

    gbuf[0:CONV_HALO, :] = gbuf[tm:tm + CONV_HALO, :]
    dbuf[0:POOL_HALO, :] = dbuf[tm:tm + POOL_HALO, :]


def _odd_in_proj(h, norm_g, w_in, conv_w, conv_b, cn_g, cn_b, pool_w, pool_scale, tm):
    bsz, seq, d = h.shape
    kern = functools.partial(_odd_in_kernel, tm=tm)
    n_groups = len(POOL_WINDOWS)
    return pl.pallas_call(
        kern,
        grid=(bsz, seq // tm),
        in_specs=[
            pl.BlockSpec((1, tm, d), lambda b, s: (b, s, 0)),
            _const_spec((1, d)),
            _const_spec((d, P_ODD)),
            _const_spec((CONV_WIDTH, C_W)),
            _const_spec((1, C_W)),
            _const_spec((1, C_W)),
            _const_spec((1, C_W)),
            _const_spec((n_groups, POOL_GROUP_DIM, POOL_GROUP_DIM)),
            _const_spec((1, D_W)),
        ],
        out_specs=[
            pl.BlockSpec((1, tm, C_W), lambda b, s: (b, s, 0)),
            pl.BlockSpec((1, tm, D_W), lambda b, s: (b, s, 0)),
        ],
        out_shape=[
            jax.ShapeDtypeStruct((bsz, seq, C_W), BF16),
            jax.ShapeDtypeStruct((bsz, seq, D_W), BF16),
        ],
        scratch_shapes=[pltpu.VMEM((tm + CONV_HALO, C_W), F32),
                        pltpu.VMEM((tm + POOL_HALO, D_W), F32),
                        pltpu.VMEM((SUBLANES, CONV_ROWS + SUBLANES, C_W), F32)],
        compiler_params=_compiler_params(2, 40),
        name="odd_in_proj",
    )(h, norm_g, w_in, conv_w, conv_b, cn_g, cn_b, pool_w, pool_scale)


def _ffn_kernel(h_ref, ya_ref, yb_ref, woa_ref, wob_ref, fg_ref, wg_ref, wu_ref, wd_ref, fin_ref,
                o_ref, *, final):
    h1 = h_ref[...] + (_dg(ya_ref[...], woa_ref[...]) + _dg(yb_ref[...], wob_ref[...]))
    xn = _rms(h1, fg_ref[...]).astype(BF16)
    acc = jnp.zeros_like(h1)
    for j in range(0, FFN_HIDDEN, FFN_HIDDEN_CHUNK):
        gt = _dg(xn, wg_ref[:, j:j + FFN_HIDDEN_CHUNK])
        up = _dg(xn, wu_ref[:, j:j + FFN_HIDDEN_CHUNK])
        act = (gt * _sigmoid(gt) * up).astype(BF16)
        acc = acc + _dg(act, wd_ref[j:j + FFN_HIDDEN_CHUNK, :])
    out = h1 + acc
    if final:
        out = _rms(out, fin_ref[...])
    o_ref[...] = out


def _out_proj_ffn(h, ya, yb, wo_a, wo_b, ffn_g, w_gate, w_up, w_down, final_g, final, tm):
    t, d = h.shape
    half = ya.shape[1]
    kern = functools.partial(_ffn_kernel, final=final)
    return pl.pallas_call(
        kern,
        grid=(t // tm,),
        in_specs=[
            pl.BlockSpec((tm, d), lambda i: (i, 0)),
            pl.BlockSpec((tm, half), lambda i: (i, 0)),
            pl.BlockSpec((tm, half), lambda i: (i, 0)),
            _const_spec((half, d)),
            _const_spec((half, d)),
            _const_spec((1, d)),
            _const_spec((d, FFN_HIDDEN)),
            _const_spec((d, FFN_HIDDEN)),
            _const_spec((FFN_HIDDEN, d)),
            _const_spec((1, d)),
        ],
        out_specs=pl.BlockSpec((tm, d), lambda i: (i, 0)),
        out_shape=jax.ShapeDtypeStruct((t, d), F32),
        compiler_params=_compiler_params(1, 52),
        name="out_proj_ffn",
    )(h, ya, yb, wo_a, wo_b, ffn_g, w_gate, w_up, w_down, final_g)


def _hi_lo(w):
    hi = w.astype(BF16)
    lo = (w - hi.astype(F32)).astype(BF16)
    return jnp.stack([hi, lo])


def kernel(x, mix_norm_g, ffn_norm_g, final_norm_g, ev_w_in, ev_shift_mu, ev_w0, ev_w_up, ev_a0, ev_a_up, ev_g_up, ev_k_k, ev_k_a, ev_r_k, ev_lnx_g, ev_lnx_b, ev_bnorm_g, ev_bnorm_b, ev_spatial_w, ev_spatial_b, ev_w_out, od_w_in, od_conv_w, od_conv_b, od_cnorm_g, od_cnorm_b, od_pool_w, od_pool_scale, od_w_out, ffn_w_gate, ffn_w_up, ffn_w_down):
    bsz, seq, d = x.shape
    tm = min(512, seq)
    row = lambda vec: vec.reshape(1, -1)
    zpad = jnp.zeros((LANES - W_LORA, A_W), F32)

    h = x
    for layer in range(DEPTH):
        i = layer // 2
        if layer % 2 == 0:
            w_in = ev_w_in[i].astype(BF16)
            pa, y2 = _even_in_proj(
                h, row(mix_norm_g[layer]), w_in[:, :P_A], w_in[:, P_A:], row(ev_shift_mu[i]),
                row(ev_bnorm_g[i]), row(ev_bnorm_b[i]), ev_spatial_w[i], ev_spatial_b[i].T, tm)
            y1 = _rwkv_mix(
                pa, row(ev_w0[i]), row(ev_a0[i]), row(ev_k_k[i]), row(ev_k_a[i]), row(ev_r_k[i]),
                row(ev_lnx_g[i]), row(ev_lnx_b[i]),
                _hi_lo(jnp.concatenate([ev_w_up[i], zpad], axis=0)),
                jnp.concatenate([zpad, ev_a_up[i]], axis=0).astype(BF16),
                ev_g_up[i].astype(BF16))
            w_out = ev_w_out[i].astype(BF16)
        else:
            y1, y2 = _odd_in_proj(
                h, row(mix_norm_g[layer]), od_w_in[i].astype(BF16), od_conv_w[i], row(od_conv_b[i]),
                row(od_cnorm_g[i]), row(od_cnorm_b[i]), od_pool_w[i], row(od_pool_scale[i]), tm)
            w_out = od_w_out[i].astype(BF16)
        half = d // 2
        h = _out_proj_ffn(
            h.reshape(bsz * seq, d), y1.reshape(bsz * seq, half), y2.reshape(bsz * seq, half),
            w_out[:half], w_out[half:], row(ffn_norm_g[layer]),
            ffn_w_gate[layer].astype(BF16), ffn_w_up[layer].astype(BF16),
            ffn_w_down[layer].astype(BF16), row(final_norm_g),
            layer == DEPTH - 1, tm).reshape(bsz, seq, d)
    return h
```

```python
import functools

import jax
import jax.numpy as jnp
from jax import lax
from jax.experimental import pallas as pl
from jax.experimental.pallas import tpu as pltpu

F32 = jnp.float32
BF16 = jnp.bfloat16

D_MODEL = 1024
DEPTH = 4
A_W = 512
HEAD_DIM = 64
W_LORA = 64
A_LORA = 64
G_LORA = 128
P_A = 3 * A_W + W_LORA + A_LORA + G_LORA
LNX_EPS = 64e-5
B_W = 512
B_GROUPS = 4
SP_CHUNK = 128
C_W = 512
CONV_WIDTH = 31
D_W = 512
POOL_WINDOWS = (2, 4, 8, 16)
POOL_GROUP_DIM = 128
P_ODD = 2 * C_W + D_W
FFN_HIDDEN = 2816
RMS_EPS = 1e-5
LN_EPS = 1e-5

LANES = 128
SCAN_CHUNK = 64
SCAN_CHUNK_LOG2 = SCAN_CHUNK.bit_length() - 1
SCAN_CHUNKS_PER_STEP = 4
CONV_HALO = 32
POOL_HALO = 16
CONV_ROWS = 128
SUBLANES = 8
FFN_HIDDEN_CHUNK = 256
MIB = 1024 * 1024

_NN = (((1,), (0,)), ((), ()))
_NT = (((1,), (1,)), ((), ()))

_PARTS = {
    "gram": (1, 1),
    "inv": (1, 1),
    "akv": (1, 1),
    "pq": (1, 1),
    "gy": (1, 1),
    "trans": (1, 1),
    "out": (1, 1),
    "state": (2, 1),
}


def _dg(a, b, dims=_NN):
    return lax.dot_general(a, b, dims, preferred_element_type=F32)


def _split2(x):
    hi = x.astype(BF16)
    lo = (x - hi.astype(F32)).astype(BF16)
    return hi, lo


def _parts(x, n):
    return (x.astype(BF16),) if n == 1 else _split2(x)


def _mmp(a, b, dims=_NN):
    if len(a) == 2:
        m = a[0].shape[0]
        both = _dg(jnp.concatenate(a, axis=0), b[0], dims)
        out = both[:m] + both[m:]
    else:
        out = _dg(a[0], b[0], dims)
    if len(b) == 2:
        out = out + _dg(a[0], b[1], dims)
    return out


def _rms(x, g):
    return x * lax.rsqrt(jnp.mean(x * x, axis=-1, keepdims=True) + RMS_EPS) * g


def _layer_norm(x, g, b, eps):
    mu = jnp.mean(x, axis=-1, keepdims=True)
    xc = x - mu
    var = jnp.mean(xc * xc, axis=-1, keepdims=True)
    return xc * lax.rsqrt(var + eps) * g + b


def _sigmoid(x):
    return jax.nn.sigmoid(x)


def _compiler_params(n_grid, vmem_mib):
    return pltpu.CompilerParams(
        dimension_semantics=("arbitrary",) * n_grid,
        vmem_limit_bytes=vmem_mib * MIB,
    )


def _const_spec(shape):
    nd = len(shape)
    return pl.BlockSpec(shape, lambda *_: (0,) * nd, pipeline_mode=pl.Buffered(1))


def _even_in_kernel(x_ref, g_ref, w_ref, mu_ref, bng_ref, bnb_ref, spw_ref, spbt_ref,
                    pa_ref, yb_ref, shift_ref, *, tm):
    @pl.when(pl.program_id(1) == 0)
    def _():
        shift_ref[0:8, :] = jnp.zeros((8, P_A), F32)

    xn = _rms(x_ref[0], g_ref[...]).astype(BF16)

    pb = _dg(xn, w_ref[:, P_A:])
    gb = 0.5 * pb * (1.0 + lax.erf(pb * (2.0 ** -0.5)))
    u = gb[:, :B_W]
    zn = _layer_norm(gb[:, B_W:], bng_ref[...], bnb_ref[...], LN_EPS).astype(BF16)
    row = lax.broadcasted_iota(jnp.int32, (SP_CHUNK, SP_CHUNK), 0)
    col = lax.broadcasted_iota(jnp.int32, (SP_CHUNK, SP_CHUNK), 1)
    tril = row >= col
    sz = {}
    for g in range(B_GROUPS):
        wg = jnp.where(tril, spw_ref[g], 0.0).astype(BF16)
        for c in range(tm // SP_CHUNK):
            sz[g, c] = _dg(wg, zn[c * SP_CHUNK:(c + 1) * SP_CHUNK, g * LANES:(g + 1) * LANES])

    pa = _dg(xn, w_ref[:, :P_A])

    for g in range(B_GROUPS):
        bias = spbt_ref[:, g:g + 1]
        cs = slice(g * LANES, (g + 1) * LANES)
        for c in range(tm // SP_CHUNK):
            rs = slice(c * SP_CHUNK, (c + 1) * SP_CHUNK)
            yb_ref[0, rs, cs] = (u[rs, cs] * (sz[g, c] + bias)).astype(BF16)

    shift_ref[8:8 + tm, :] = pa
    shifted = shift_ref[7:7 + tm, :]
    pa_ref[0] = pa + (shifted - pa) * mu_ref[...]
    shift_ref[7:8, :] = pa[tm - 1:tm, :]


def _even_in_proj(h, norm_g, w_in, mu, bn_g, bn_b, sp_w, sp_bt, tm):
    bsz, seq, d = h.shape
    kern = functools.partial(_even_in_kernel, tm=tm)
    return pl.pallas_call(
        kern,
        grid=(bsz, seq // tm),
        in_specs=[
            pl.BlockSpec((1, tm, d), lambda b, s: (b, s, 0)),
            _const_spec((1, d)),
            _const_spec((d, P_A + 2 * B_W)),
            _const_spec((1, P_A)),
            _const_spec((1, B_W)),
            _const_spec((1, B_W)),
            _const_spec((B_GROUPS, SP_CHUNK, SP_CHUNK)),
            _const_spec((SP_CHUNK, B_GROUPS)),
        ],
        out_specs=[
            pl.BlockSpec((1, tm, P_A), lambda b, s: (b, s, 0)),
            pl.BlockSpec((1, tm, B_W), lambda b, s: (b, s, 0)),
        ],
        out_shape=[
            jax.ShapeDtypeStruct((bsz, seq, P_A), F32),
            jax.ShapeDtypeStruct((bsz, seq, B_W), BF16),
        ],
        scratch_shapes=[pltpu.VMEM((tm + 8, P_A), F32)],
        compiler_params=_compiler_params(2, 48),
        name="even_in_proj",
    )(h, norm_g, w_in, mu, bn_g, bn_b, sp_w, sp_bt)


def _rwkv_kernel(pa_ref, w0_ref, a0_ref, kk_ref, ka_ref, rk_ref, lg_ref, lb_ref,
                 wup_ref, aup_ref, gup_ref, y_ref, s_ref, *, nch):
    C = SCAN_CHUNK
    tb = nch * C
    n_pairs = A_W // LANES

    @pl.when(pl.program_id(1) == 0)
    def _():
        s_ref[...] = jnp.zeros_like(s_ref)

    pa = pa_ref[0]
    r = pa[:, 0:A_W]
    k = pa[:, A_W:2 * A_W]
    v = pa[:, 2 * A_W:3 * A_W]
    wa_d = pa[:, 3 * A_W:3 * A_W + LANES]
    gd = pa[:, 3 * A_W + LANES:P_A]

    wlin = w0_ref[...] + _mmp(_split2(jnp.tanh(wa_d)), (wup_ref[0], wup_ref[1]))
    ld = -(2.718281828459045 ** -0.5) * _sigmoid(wlin)
    a = _sigmoid(a0_ref[...] + _dg(wa_d.astype(BF16), aup_ref[...]))
    gate = _dg(_sigmoid(gd).astype(BF16), gup_ref[...])

    row = lax.broadcasted_iota(jnp.int32, (C, LANES), 0)
    lane = lax.broadcasted_iota(jnp.int32, (C, LANES), 1)
    lane_in = jnp.bitwise_and(lane, HEAD_DIM - 1)
    m0 = lane < HEAD_DIM
    strict = lane_in < row
    incl = lane_in <= row
    eye_pair = jnp.where(lane_in == row, 1.0, 0.0)
    row2 = lax.broadcasted_iota(jnp.int32, (2 * C, LANES), 0)
    lane2 = lax.broadcasted_iota(jnp.int32, (2 * C, LANES), 1)
    mdiag = (row2 < C) == (lane2 < HEAD_DIM)
    mdiag_wide = jnp.concatenate([mdiag, mdiag], axis=1)
    eye2 = jnp.where(row2 == lane2, 1.0, 0.0)
    row4 = lax.broadcasted_iota(jnp.int32, (4 * C, LANES), 0)
    lane4 = lax.broadcasted_iota(jnp.int32, (4 * C, LANES), 1)
    mask4 = (jnp.bitwise_and(jnp.right_shift(row4, SCAN_CHUNK_LOG2), 1) == 0) == (lane4 < HEAD_DIM)
    zeros_c = jnp.zeros((C, LANES), F32)
    ones_pair = jnp.where(mdiag, 1.0, 0.0).astype(BF16)

    def head_sum(x):
        xb = x.astype(BF16)
        stacked = jnp.concatenate([xb[:, p * LANES:(p + 1) * LANES] for p in range(n_pairs)], axis=0)
        s = _dg(stacked, ones_pair)
        return jnp.concatenate([s[p * tb:(p + 1) * tb] for p in range(n_pairs)], axis=1)

    kk = k * kk_ref[...]
    kkn = kk * jnp.minimum(lax.rsqrt(head_sum(kk * kk)), 1e12)
    k2 = k * (1.0 + (a - 1.0) * ka_ref[...])

    rc = lax.broadcasted_iota(jnp.int32, (tb, tb), 0)
    cc = lax.broadcasted_iota(jnp.int32, (tb, tb), 1)
    same_chunk = jnp.right_shift(rc, SCAN_CHUNK_LOG2) == jnp.right_shift(cc, SCAN_CHUNK_LOG2)
    lower_ones = jnp.where(jnp.logical_and(rc >= cc, same_chunk), 1.0, 0.0).astype(BF16)
    ldp = _split2(ld)
    cum = _dg(lower_ones, ldp[0]) + _dg(lower_ones, ldp[1])
    ecum = jnp.exp(cum)
    einv = jnp.exp(-cum)
    eprev = jnp.exp(cum - ld)
    rt = r * ecum
    at = -(kkn * eprev)
    bt = kkn * a * einv
    kt = k2 * einv

    def bd(x, n):
        return _parts(jnp.where(mdiag, jnp.concatenate([x, x], axis=0), 0.0), n)

    chains = [(j, p) for j in range(nch) for p in range(n_pairs)]

    def tile(x, jp):
        j, p = jp
        return x[j * C:(j + 1) * C, p * LANES:(p + 1) * LANES]

    at_c = [tile(at, jp) for jp in chains]
    rt_c = [tile(rt, jp) for jp in chains]
    bt_c = [tile(bt, jp) for jp in chains]
    kt_c = [tile(kt, jp) for jp in chains]
    v_c = [tile(v, jp) for jp in chains]

    nl, nr = _PARTS["gram"]
    gram = [
        _mmp(_parts(jnp.concatenate([at_c[i], rt_c[i]], axis=0), nl),
             _parts(jnp.where(mask4, jnp.concatenate([bt_c[i], bt_c[i], kt_c[i], kt_c[i]], axis=0), 0.0), nr),
             _NT)
        for i in range(len(chains))]
    a_ab = [jnp.where(strict, g[0:C, 0:LANES], 0.0) for g in gram]
    a_ak = [jnp.where(strict, g[0:C, LANES:], 0.0) for g in gram]
    m_rb = [jnp.where(incl, g[C:, 0:LANES], 0.0) for g in gram]
    m_rk = [jnp.where(incl, g[C:, LANES:], 0.0) for g in gram]

    nl, nr = _PARTS["akv"]
    akv = [_mmp(_parts(x, nl), bd(y, nr)) for x, y in zip(a_ak, v_c)]

    nl, nr = _PARTS["inv"]
    t_inv = [eye_pair + x for x in a_ab]
    pw = [_mmp(_parts(x, nl), bd(x, nr)) for x in a_ab]
    for _ in range(SCAN_CHUNK_LOG2 - 2):
        res = [_mmp(_parts(jnp.concatenate([t, w], axis=0), nl), bd(w, nr)) for t, w in zip(t_inv, pw)]
        t_inv = [t + x[0:C] for t, x in zip(t_inv, res)]
        pw = [x[C:] for x in res]
    t_inv = [t + _mmp(_parts(t, nl), bd(w, nr)) for t, w in zip(t_inv, pw)]

    nl, nr = _PARTS["pq"]
    pq = [
        _mmp(_parts(t, nl),
             _parts(jnp.where(mdiag_wide, jnp.concatenate([jnp.concatenate([x, y], axis=1)] * 2, axis=0), 0.0), nr))
        for t, x, y in zip(t_inv, at_c, akv)]
    nl, nr = _PARTS["gy"]
    gy = [
        _mmp(_parts(m, nl), _parts(jnp.where(mdiag_wide, jnp.concatenate([x, x], axis=0), 0.0), nr))
        for m, x in zip(m_rb, pq)]
    g_mat = [x + g[:, 0:LANES] for x, g in zip(rt_c, gy)]
    y_in = [g[:, LANES:] + _mmp(_parts(m, nl), bd(y, nr)) for g, m, y in zip(gy, m_rk, v_c)]

    nl, nr = _PARTS["trans"]
    phi_t, psi_t = [], []
    for i, (j, p) in enumerate(chains):
        pt0 = jnp.concatenate([pq[i][:, 0:LANES], zeros_c], axis=0).T
        qvt = jnp.concatenate([pq[i][:, LANES:], v_c[i]], axis=0).T
        bk = _parts(jnp.concatenate([bt_c[i], kt_c[i]], axis=0), nr)
        full = _mmp(_parts(jnp.concatenate([pt0, qvt], axis=0), nl), bk)
        w_end = ecum[(j + 1) * C - 1:(j + 1) * C, p * LANES:(p + 1) * LANES]
        phi_t.append(jnp.where(mdiag, eye2 + full[0:2 * C], 0.0) * w_end)
        psi_t.append(jnp.where(m0, full[2 * C:3 * C], full[3 * C:]) * w_end)

    state = [s_ref[p] for p in range(n_pairs)]
    ol, orr = _PARTS["out"]
    sl, sr = _PARTS["state"]
    y_rows = []
    for j in range(nch):
        idx = [j * n_pairs + p for p in range(n_pairs)]
        y_rows.append(jnp.concatenate(
            [_mmp(_parts(g_mat[i], ol), bd(state[p], orr), _NT) + y_in[i] for p, i in enumerate(idx)], axis=1))
        state = [_mmp(_parts(state[p], sl), _parts(phi_t[i], sr)) + psi_t[i] for p, i in enumerate(idx)]
    for p in range(n_pairs):
        s_ref[p] = state[p]

    y = jnp.concatenate(y_rows, axis=0)
    inv_n = 1.0 / HEAD_DIM
    yc = y - head_sum(y) * inv_n
    yv = head_sum(yc * yc) * inv_n
    yn = yc * lax.rsqrt(yv + LNX_EPS) * lg_ref[...] + lb_ref[...]
    bonus = head_sum(r * k2 * rk_ref[...]) * v
    y_ref[0] = ((yn + bonus) * gate).astype(BF16)


def _rwkv_mix(pa, w0, a0, k_k, k_a, r_k, lnx_g, lnx_b, wup, aup, gup):
    bsz, seq, _ = pa.shape
    nch = SCAN_CHUNKS_PER_STEP
    tb = nch * SCAN_CHUNK
    vec = _const_spec((1, A_W))
    lora = _const_spec((LANES, A_W))
    return pl.pallas_call(
        functools.partial(_rwkv_kernel, nch=nch),
        grid=(bsz, seq // tb),
        in_specs=[pl.BlockSpec((1, tb, P_A), lambda b, c: (b, c, 0)),
                  vec, vec, vec, vec, vec, vec, vec, _const_spec((2, LANES, A_W)), lora, lora],
        out_specs=pl.BlockSpec((1, tb, A_W), lambda b, c: (b, c, 0)),
        out_shape=jax.ShapeDtypeStruct((bsz, seq, A_W), BF16),
        scratch_shapes=[pltpu.VMEM((A_W // LANES, SCAN_CHUNK, LANES), F32)],
        compiler_params=_compiler_params(2, 32),
        name="rwkv7_mix",
    )(pa, w0, a0, k_k, k_a, r_k, lnx_g, lnx_b, wup, aup, gup)


def _odd_in_kernel(x_ref, g_ref, w_ref, cw_ref, cb_ref, cng_ref, cnb_ref, pw_ref, ps_ref,
                   yc_ref, yd_ref, gbuf, dbuf, pbuf, *, tm):
    s = pl.program_id(1)

    @pl.when(s == 0)
    def _():
        gbuf[0:CONV_HALO, :] = jnp.zeros((CONV_HALO, C_W), F32)
        dbuf[0:POOL_HALO, :] = jnp.zeros((POOL_HALO, D_W), F32)

    xn = _rms(x_ref[0], g_ref[...]).astype(BF16)
    proj = _dg(xn, w_ref[...])
    glu = proj[:, :C_W] * _sigmoid(proj[:, C_W:2 * C_W])
    d = proj[:, 2 * C_W:]
    gbuf[CONV_HALO:CONV_HALO + tm, :] = glu
    dbuf[POOL_HALO:POOL_HALO + tm, :] = d

    first = CONV_HALO - (CONV_WIDTH - 1)
    for rt in range(tm // CONV_ROWS):
        base = rt * CONV_ROWS
        acc = jnp.broadcast_to(cb_ref[...], (CONV_ROWS, C_W))
        for r in range(SUBLANES):
            rows = CONV_ROWS if r == 0 else CONV_ROWS + SUBLANES
            part = None
            for o in range(first, first + CONV_WIDTH):
                if o % SUBLANES != r:
                    continue
                lo = base + o - r
                term = gbuf[lo:lo + rows, :] * cw_ref[o - first:o - first + 1, :]
                part = term if part is None else part + term
            if r == 0:
                acc = acc + part
            else:
                pbuf[r, 0:rows, :] = part
                acc = acc + pbuf[r, r:r + CONV_ROWS, :]
        ln = _layer_norm(acc, cng_ref[...], cnb_ref[...], LN_EPS)
        yc_ref[0, base:base + CONV_ROWS, :] = (ln * _sigmoid(ln)).astype(BF16)

    t_idx = s * tm + lax.broadcasted_iota(jnp.int32, (tm, 1), 0)
    for gi, win in enumerate(POOL_WINDOWS):
        cs = slice(gi * POOL_GROUP_DIM, (gi + 1) * POOL_GROUP_DIM)
        tot = dbuf[POOL_HALO:POOL_HALO + tm, cs]
        for i in range(1, win):
            tot = tot + dbuf[POOL_HALO - i:POOL_HALO - i + tm, cs]
        inv_count = 1.0 / jnp.minimum(t_idx + 1, win).astype(F32)
        pooled = tot * inv_count - d[:, cs]
        yd = _dg(pooled.astype(BF16), pw_ref[gi].astype(BF16)) * ps_ref[:, cs]
        yd_ref[0, :, cs] = yd.astype(BF16)

    gbuf[0:CONV_HALO, :] = gbuf[tm:tm + CONV_HALO, :]
    dbuf[0:POOL_HALO, :] = dbuf[tm:tm + POOL_HALO, :]


def _odd_in_proj(h, norm_g, w_in, conv_w, conv_b, cn_g, cn_b, pool_w, pool_scale, tm):
    bsz, seq, d = h.shape
    kern = functools.partial(_odd_in_kernel, tm=tm)
    n_groups = len(POOL_WINDOWS)
    return pl.pallas_call(
        kern,
        grid=(bsz, seq // tm),
        in_specs=[
            pl.BlockSpec((1, tm, d), lambda b, s: (b, s, 0)),
            _const_spec((1, d)),
            _const_spec((d, P_ODD)),
            _const_spec((CONV_WIDTH, C_W)),
            _const_spec((1, C_W)),
            _const_spec((1, C_W)),
            _const_spec((1, C_W)),
            _const_spec((n_groups, POOL_GROUP_DIM, POOL_GROUP_DIM)),
            _const_spec((1, D_W)),
        ],
        out_specs=[
            pl.BlockSpec((1, tm, C_W), lambda b, s: (b, s, 0)),
            pl.BlockSpec((1, tm, D_W), lambda b, s: (b, s, 0)),
        ],
        out_shape=[
            jax.ShapeDtypeStruct((bsz, seq, C_W), BF16),
            jax.ShapeDtypeStruct((bsz, seq, D_W), BF16),
        ],
        scratch_shapes=[pltpu.VMEM((tm + CONV_HALO, C_W), F32),
                        pltpu.VMEM((tm + POOL_HALO, D_W), F32),
                        pltpu.VMEM((SUBLANES, CONV_ROWS + SUBLANES, C_W), F32)],
        compiler_params=_compiler_params(2, 40),
        name="odd_in_proj",
    )(h, norm_g, w_in, conv_w, conv_b, cn_g, cn_b, pool_w, pool_scale)


def _ffn_kernel(h_ref, ya_ref, yb_ref, wo_ref, fg_ref, wg_ref, wu_ref, wd_ref, fin_ref,
                o_ref, *, final):
    half = ya_ref.shape[1]
    h1 = h_ref[...] + (_dg(ya_ref[...], wo_ref[0:half, :]) + _dg(yb_ref[...], wo_ref[half:, :]))
    xn = _rms(h1, fg_ref[...]).astype(BF16)
    acc = jnp.zeros_like(h1)
    for j in range(0, FFN_HIDDEN, FFN_HIDDEN_CHUNK):
        gt = _dg(xn, wg_ref[:, j:j + FFN_HIDDEN_CHUNK])
        up = _dg(xn, wu_ref[:, j:j + FFN_HIDDEN_CHUNK])
        act = (gt * _sigmoid(gt) * up).astype(BF16)
        acc = acc + _dg(act, wd_ref[j:j + FFN_HIDDEN_CHUNK, :])
    out = h1 + acc
    if final:
        out = _rms(out, fin_ref[...])
    o_ref[...] = out


def _out_proj_ffn(h, ya, yb, w_out, ffn_g, w_gate, w_up, w_down, final_g, final, tm):
    t, d = h.shape
    half = ya.shape[1]
    kern = functools.partial(_ffn_kernel, final=final)
    return pl.pallas_call(
        kern,
        grid=(t // tm,),
        in_specs=[
            pl.BlockSpec((tm, d), lambda i: (i, 0)),
            pl.BlockSpec((tm, half), lambda i: (i, 0)),
            pl.BlockSpec((tm, half), lambda i: (i, 0)),
            _const_spec((2 * half, d)),
            _const_spec((1, d)),
            _const_spec((d, FFN_HIDDEN)),
            _const_spec((d, FFN_HIDDEN)),
            _const_spec((FFN_HIDDEN, d)),
            _const_spec((1, d)),
        ],
        out_specs=pl.BlockSpec((tm, d), lambda i: (i, 0)),
        out_shape=jax.ShapeDtypeStruct((t, d), F32),
        compiler_params=_compiler_params(1, 52),
        name="out_proj_ffn",
    )(h, ya, yb, w_out, ffn_g, w_gate, w_up, w_down, final_g)


def _hi_lo(w):
    hi = w.astype(BF16)
    lo = (w - hi.astype(F32)).astype(BF16)
    return jnp.stack([hi, lo])


def kernel(x, mix_norm_g, ffn_norm_g, final_norm_g, ev_w_in, ev_shift_mu, ev_w0, ev_w_up, ev_a0, ev_a_up, ev_g_up, ev_k_k, ev_k_a, ev_r_k, ev_lnx_g, ev_lnx_b, ev_bnorm_g, ev_bnorm_b, ev_spatial_w, ev_spatial_b, ev_w_out, od_w_in, od_conv_w, od_conv_b, od_cnorm_g, od_cnorm_b, od_pool_w, od_pool_scale, od_w_out, ffn_w_gate, ffn_w_up, ffn_w_down):
    bsz, seq, d = x.shape
    tm = min(512, seq)
    row = lambda vec: vec.reshape(1, -1)
    zpad = jnp.zeros((LANES - W_LORA, A_W), F32)

    h = x
    for layer in range(DEPTH):
        i = layer // 2
        if layer % 2 == 0:
            w_in = ev_w_in[i].astype(BF16)
            pa, y2 = _even_in_proj(
                h, row(mix_norm_g[layer]), w_in, row(ev_shift_mu[i]),
                row(ev_bnorm_g[i]), row(ev_bnorm_b[i]), ev_spatial_w[i], ev_spatial_b[i].T, tm)
            y1 = _rwkv_mix(
                pa, row(ev_w0[i]), row(ev_a0[i]), row(ev_k_k[i]), row(ev_k_a[i]), row(ev_r_k[i]),
                row(ev_lnx_g[i]), row(ev_lnx_b[i]),
                _hi_lo(jnp.concatenate([ev_w_up[i], zpad], axis=0)),
                jnp.concatenate([zpad, ev_a_up[i]], axis=0).astype(BF16),
                ev_g_up[i].astype(BF16))
            w_out = ev_w_out[i].astype(BF16)
        else:
            y1, y2 = _odd_in_proj(
                h, row(mix_norm_g[layer]), od_w_in[i].astype(BF16), od_conv_w[i], row(od_conv_b[i]),
                row(od_cnorm_g[i]), row(od_cnorm_b[i]), od_pool_w[i], row(od_pool_scale[i]), tm)
            w_out = od_w_out[i].astype(BF16)
        half = d // 2
        h = _out_proj_ffn(
            h.reshape(bsz * seq, d), y1.reshape(bsz * seq, half), y2.reshape(bsz * seq, half),
            w_out, row(ffn_norm_g[layer]),
            ffn_w_gate[layer].astype(BF16), ffn_w_up[layer].astype(BF16),
            ffn_w_down[layer].astype(BF16), row(final_norm_g),
            layer == DEPTH - 1, tm).reshape(bsz, seq, d)
    return h
```
